```python
import math
import jax, jax.numpy as jnp
from jax import lax
import numpy as np

D_MODEL = 2048
BATCH = 2
SEQ = 4096
DEPTH = 4
DEC_BATCH = 8
DEC_SEQ = 8
PAST_LEN = 16384
PAGE_SIZE = 128

HEAD_DIM = 128
H_A = 8
KV_A = 2
H_I = 16
D_I = 64
TOPK_MAX = 256
H_B = 8
NUM_BUCKETS = 32
MAX_DISTANCE = 128
D_FF = 256 * ((8 * D_MODEL // 3 + 255) // 256)
CONV_W = 3
Q_BLOCK = 128
LN_EPS = 1e-5
DN_ALPHA = (2 * DEPTH) ** 0.25
DN_BETA = (8 * DEPTH) ** -0.25
PROJ_SIZES = (H_A * HEAD_DIM, KV_A * HEAD_DIM, KV_A * HEAD_DIM, H_I * D_I, D_I, H_I,
              H_B * HEAD_DIM, H_B * HEAD_DIM, H_B * HEAD_DIM, 2 * D_MODEL)
P_TOTAL = sum(PROJ_SIZES)

kernel_name = "dsa_stickbreak_griffin_deepnorm_step"


def layer_norm(x, g, b):
    xf = x.astype(jnp.float32)
    mu = jnp.mean(xf, axis=-1, keepdims=True)
    xc = xf - mu
    var = jnp.mean(xc * xc, axis=-1, keepdims=True)
    return (xc * lax.rsqrt(var + LN_EPS) * g + b).astype(x.dtype)


def split_proj(x, w):
    n, t, _ = x.shape
    z = jnp.einsum('btd,dp->btp', x, w)
    offs = np.cumsum(PROJ_SIZES)[:-1].tolist()
    q_a, k_a, v_a, q_i, k_i, w_i, q_b, k_b, v_b, g = jnp.split(z, offs, axis=-1)
    hd = (n, t, -1, HEAD_DIM)
    return (q_a.reshape(hd), k_a.reshape(hd), v_a.reshape(hd), q_i.reshape(n, t, H_I, D_I),
            k_i, w_i, q_b.reshape(hd), k_b.reshape(hd), v_b.reshape(hd), g)


def t5_bucket(rel):
    n = jnp.maximum(rel, 0)
    max_exact = NUM_BUCKETS // 2
    nf = jnp.maximum(n, 1).astype(jnp.float32)
    large = max_exact + (jnp.log(nf / max_exact) / math.log(MAX_DISTANCE / max_exact)
                         * (NUM_BUCKETS - max_exact)).astype(jnp.int32)
    large = jnp.minimum(large, NUM_BUCKETS - 1)
    return jnp.where(n < max_exact, n, large)


def to_blocks(x):
    n, t = x.shape[:2]
    return jnp.swapaxes(x.reshape(n, t // Q_BLOCK, Q_BLOCK, *x.shape[2:]), 0, 1)


def from_blocks(y):
    nb, n, qb = y.shape[:3]
    return jnp.swapaxes(y, 0, 1).reshape(n, nb * qb, *y.shape[3:])


def indexer_scores(q_i, w_i, k_i):
    s = jnp.einsum('bthd,bsd->bths', q_i, k_i).astype(jnp.float32) * D_I ** -0.5
    return jnp.einsum('bths,bth->bts', jax.nn.relu(s), w_i.astype(jnp.float32)) * H_I ** -0.5


def dsa_attend(q, k_sel, v_sel, sel_pos, q_pos, rel_bias):
    n, tq, h, dh = q.shape
    g = k_sel.shape[3]
    r = h // g
    kk = k_sel.shape[2]
    qg = q.reshape(n, tq, g, r, dh)
    logits = jnp.einsum('btgrd,btkgd->btgrk', qg, k_sel).astype(jnp.float32) * dh ** -0.5
    rel = q_pos[None, :, None] - sel_pos
    valid = rel >= 0
    bias = rel_bias[t5_bucket(rel)].astype(jnp.float32)
    bias = bias.reshape(n, tq, kk, g, r).transpose(0, 1, 3, 4, 2)
    logits = jnp.where(valid[:, :, None, None, :], logits + bias, -jnp.inf)
    p = jax.nn.softmax(logits, axis=-1)
    o = jnp.einsum('btgrk,btkgd->btgrd', p.astype(v_sel.dtype), v_sel)
    return o.reshape(n, tq, h * dh)


def dsa_prompt(q_a, k_a, v_a, q_i, k_i, w_i, rel_bias):
    n, t = q_a.shape[:2]
    topk = min(TOPK_MAX, t // 4)
    nb = t // Q_BLOCK
    k_pos = jnp.arange(t, dtype=jnp.int32)
    gather_rows = jax.vmap(lambda rows, ii: rows[ii])

    def block(args):
        qa_b, qi_b, wi_b, b0 = args
        q_pos = b0 * Q_BLOCK + jnp.arange(Q_BLOCK, dtype=jnp.int32)
        s = indexer_scores(qi_b, wi_b, k_i)
        s = jnp.where(k_pos[None, None, :] <= q_pos[None, :, None], s, -jnp.inf)
        _, idx = lax.top_k(s, topk)
        return dsa_attend(qa_b, gather_rows(k_a, idx), gather_rows(v_a, idx), idx, q_pos, rel_bias)

    out = lax.map(block, (to_blocks(q_a), to_blocks(q_i), to_blocks(w_i), jnp.arange(nb, dtype=jnp.int32)))
    return from_blocks(out)


def dsa_sample(q_a, k_new, v_new, q_i, ki_new, w_i, pool_k, pool_v, pool_ki, page_table, rel_bias):
    n, t = q_a.shape[:2]
    past = page_table.shape[1] * pool_k.shape[1]
    total = past + t
    topk = min(TOPK_MAX, total // 4)
    ki_past = pool_ki[page_table].reshape(n, past, D_I)
    ki_all = jnp.concatenate([ki_past, ki_new.astype(ki_past.dtype)], axis=1)
    q_pos = past + jnp.arange(t, dtype=jnp.int32)
    k_pos = jnp.arange(total, dtype=jnp.int32)
    s = indexer_scores(q_i, w_i, ki_all)
    s = jnp.where(k_pos[None, None, :] <= q_pos[None, :, None], s, -jnp.inf)
    _, idx = lax.top_k(s, topk)
    in_past = idx < past
    pidx = jnp.minimum(idx, past - 1)
    phys = jax.vmap(lambda pt, ii: pt[ii])(page_table, pidx // PAGE_SIZE)
    off = pidx % PAGE_SIZE
    nidx = jnp.clip(idx - past, 0, t - 1)
    gather_rows = jax.vmap(lambda rows, ii: rows[ii])
    sel = in_past[..., None, None]
    k_sel = jnp.where(sel, pool_k[phys, off], gather_rows(k_new.astype(pool_k.dtype), nidx))
    v_sel = jnp.where(sel, pool_v[phys, off], gather_rows(v_new.astype(pool_v.dtype), nidx))
    return dsa_attend(q_a, k_sel, v_sel, idx, q_pos, rel_bias)


def sb_attend(q, k, v, q_pos, k_pos):
    n, tq, h, dh = q.shape
    z = jnp.einsum('bthd,bshd->bhts', q, k).astype(jnp.float32) * dh ** -0.5
    mask = k_pos[None, :] < q_pos[:, None]
    log_keep_all = jax.nn.log_sigmoid(-z)
    log_beta = z + log_keep_all
    log_keep = jnp.where(mask, log_keep_all, 0.0)
    after = lax.cumsum(log_keep, axis=3, reverse=True) - log_keep
    a = jnp.where(mask, jnp.exp(log_beta + after), 0.0)
    o = jnp.einsum('bhts,bshd->bthd', a.astype(v.dtype), v)
    return o.reshape(n, tq, h * dh)


def sb_prompt(q, k, v):
    n, t = q.shape[:2]
    nb = t // Q_BLOCK
    k_pos = jnp.arange(t, dtype=jnp.int32)

    def block(args):
        qb, b0 = args
        q_pos = b0 * Q_BLOCK + jnp.arange(Q_BLOCK, dtype=jnp.int32)
        return sb_attend(qb, k, v, q_pos, k_pos)

    return from_blocks(lax.map(block, (to_blocks(q), jnp.arange(nb, dtype=jnp.int32))))


def sb_sample(q, k_new, v_new, pool_k, pool_v, page_table):
    n, t = q.shape[:2]
    past = page_table.shape[1] * pool_k.shape[1]
    k_all = jnp.concatenate([pool_k[page_table].reshape(n, past, H_B, HEAD_DIM), k_new.astype(pool_k.dtype)], axis=1)
    v_all = jnp.concatenate([pool_v[page_table].reshape(n, past, H_B, HEAD_DIM), v_new.astype(pool_v.dtype)], axis=1)
    q_pos = past + jnp.arange(t, dtype=jnp.int32)
    k_pos = jnp.arange(past + t, dtype=jnp.int32)
    return sb_attend(q, k_all, v_all, q_pos, k_pos)


def merge_branches(o_a, o_b, g, w_pa, w_pb, w_o):
    g_a, g_b = jnp.split(g, 2, axis=-1)
    m = (jax.nn.sigmoid(g_a) * jnp.einsum('btc,cd->btd', o_a, w_pa)
         + jax.nn.sigmoid(g_b) * jnp.einsum('btc,cd->btd', o_b, w_pb))
    return jnp.einsum('btd,de->bte', m, w_o)


def conv_ffn(x, prev, w_up, conv_w, conv_b, w_down):
    t = x.shape[1]
    u = jnp.einsum('btd,df->btf', x, w_up)
    a, val = jnp.split(u, 2, axis=-1)
    ext = jnp.concatenate([prev.astype(a.dtype), a], axis=1)
    c = conv_b + sum(conv_w[i] * ext[:, i:i + t] for i in range(CONV_W))
    h = jax.nn.gelu(c) * val
    return jnp.einsum('btf,fd->btd', h, w_down), ext[:, -(CONV_W - 1):]


def setup_inputs(seed: int = 0) -> dict:
    key = jax.random.key(seed)
    ks = jax.random.split(key, 24)
    f32 = jnp.float32
    n_pages = PAST_LEN // PAGE_SIZE
    n_used = DEC_BATCH * n_pages
    n_pool = n_used + max(1, n_used // 4)
    nrm = lambda k, shape, s=1.0: jax.random.normal(k, shape, f32) * s
    page_table = jax.random.permutation(ks[7], n_pool)[:n_used].reshape(DEC_BATCH, n_pages).astype(jnp.int32)
    return {
        "x_prompt": nrm(ks[0], (BATCH, SEQ, D_MODEL)),
        "x_sample": nrm(ks[1], (DEC_BATCH, DEC_SEQ, D_MODEL)),
        "cache_k_a": nrm(ks[2], (DEPTH, n_pool, PAGE_SIZE, KV_A, HEAD_DIM)),
        "cache_v_a": nrm(ks[3], (DEPTH, n_pool, PAGE_SIZE, KV_A, HEAD_DIM)),
        "cache_k_idx": nrm(ks[4], (DEPTH, n_pool, PAGE_SIZE, D_I)),
        "cache_k_b": nrm(ks[5], (DEPTH, n_pool, PAGE_SIZE, H_B, HEAD_DIM)),
        "cache_v_b": nrm(ks[6], (DEPTH, n_pool, PAGE_SIZE, H_B, HEAD_DIM)),
        "state_conv": nrm(ks[8], (DEPTH, DEC_BATCH, CONV_W - 1, D_FF)),
        "page_table": page_table,
        "rel_bias": nrm(ks[9], (NUM_BUCKETS, H_A), 0.5),
        "w_in": nrm(ks[10], (DEPTH, D_MODEL, P_TOTAL), D_MODEL ** -0.5),
        "w_proj_a": nrm(ks[11], (DEPTH, H_A * HEAD_DIM, D_MODEL), (H_A * HEAD_DIM) ** -0.5),
        "w_proj_b": nrm(ks[12], (DEPTH, H_B * HEAD_DIM, D_MODEL), (H_B * HEAD_DIM) ** -0.5),
        "w_out": nrm(ks[13], (DEPTH, D_MODEL, D_MODEL), D_MODEL ** -0.5 * DN_BETA),
        "ln1_g": 1.0 + nrm(ks[14], (DEPTH, D_MODEL), 0.02),
        "ln1_b": nrm(ks[15], (DEPTH, D_MODEL), 0.02),
        "w_up": nrm(ks[16], (DEPTH, D_MODEL, 2 * D_FF), D_MODEL ** -0.5),
        "conv_w": nrm(ks[17], (DEPTH, CONV_W, D_FF), CONV_W ** -0.5),
        "conv_b": nrm(ks[18], (DEPTH, D_FF), 0.02),
        "w_down": nrm(ks[19], (DEPTH, D_FF, D_MODEL), D_FF ** -0.5 * DN_BETA),
        "ln2_g": 1.0 + nrm(ks[20], (DEPTH, D_MODEL), 0.02),
        "ln2_b": nrm(ks[21], (DEPTH, D_MODEL), 0.02),
    }


def reference(x_prompt, x_sample, cache_k_a, cache_v_a, cache_k_idx, cache_k_b, cache_v_b, state_conv,
              page_table, rel_bias, w_in, w_proj_a, w_proj_b, w_out, ln1_g, ln1_b, w_up, conv_w, conv_b,
              w_down, ln2_g, ln2_b):
    n_p = x_prompt.shape[0]
    xp, xs = x_prompt, x_sample
    pk_a, pv_a, pk_i, pk_b, pv_b, pconv = [], [], [], [], [], []
    sk_a, sv_a, sk_i, sk_b, sv_b, sconv = [], [], [], [], [], []
    for l in range(DEPTH):
        qa, ka, va, qi, ki, wi, qb, kb, vb, g = split_proj(xp, w_in[l])
        oa = dsa_prompt(qa, ka, va, qi, ki, wi, rel_bias)
        ob = sb_prompt(qb, kb, vb)
        xp = layer_norm(DN_ALPHA * xp + merge_branches(oa, ob, g, w_proj_a[l], w_proj_b[l], w_out[l]),
                        ln1_g[l], ln1_b[l])
        zero_buf = jnp.zeros((n_p, CONV_W - 1, D_FF), xp.dtype)
        f, cbuf = conv_ffn(xp, zero_buf, w_up[l], conv_w[l], conv_b[l], w_down[l])
        xp = layer_norm(DN_ALPHA * xp + f, ln2_g[l], ln2_b[l])
        pk_a.append(ka); pv_a.append(va); pk_i.append(ki); pk_b.append(kb); pv_b.append(vb); pconv.append(cbuf)

        qa, ka, va, qi, ki, wi, qb, kb, vb, g = split_proj(xs, w_in[l])
        oa = dsa_sample(qa, ka, va, qi, ki, wi, cache_k_a[l], cache_v_a[l], cache_k_idx[l], page_table, rel_bias)
        ob = sb_sample(qb, kb, vb, cache_k_b[l], cache_v_b[l], page_table)
        xs = layer_norm(DN_ALPHA * xs + merge_branches(oa, ob, g, w_proj_a[l], w_proj_b[l], w_out[l]),
                        ln1_g[l], ln1_b[l])
        f, cbuf = conv_ffn(xs, state_conv[l], w_up[l], conv_w[l], conv_b[l], w_down[l])
        xs = layer_norm(DN_ALPHA * xs + f, ln2_g[l], ln2_b[l])
        sk_a.append(ka); sv_a.append(va); sk_i.append(ki); sk_b.append(kb); sv_b.append(vb); sconv.append(cbuf)

    return (xp, xs,
            jnp.stack(pk_a), jnp.stack(pv_a), jnp.stack(pk_i), jnp.stack(pk_b), jnp.stack(pv_b), jnp.stack(pconv),
            jnp.stack(sk_a), jnp.stack(sv_a), jnp.stack(sk_i), jnp.stack(sk_b), jnp.stack(sv_b), jnp.stack(sconv))
```

```python
import functools
import math

import numpy as np
import jax
import jax.numpy as jnp
from jax import lax
from jax.experimental import pallas as pl
from jax.experimental.pallas import tpu as pltpu

F32 = jnp.float32
BF16 = jnp.bfloat16
I32 = jnp.int32

HEAD_DIM = 128
H_A = 8
KV_A = 2
H_I = 16
D_I = 64
TOPK_MAX = 256
H_B = 8
NUM_BUCKETS = 32
MAX_DISTANCE = 128
CONV_W = 3
LN_EPS = 1e-5

LANES = 128
SUBLANES = 8
VMEM_LIMIT = 56 * 1024 * 1024

_QA = H_A * HEAD_DIM
_QI = H_I * D_I
_QB = H_B * HEAD_DIM
OFF_QA = 0
OFF_QI = OFF_QA + _QA
OFF_QB = OFF_QI + _QI
OFF_KB = OFF_QB + _QB
OFF_VB = OFF_KB + _QB
SMALL_W = 4 * LANES
INT_MIN = -2 ** 31
MASKED = -1e30
M_INIT = -1e20
SB_SKIP = -105.0


def _cparams(sem):
    return pltpu.CompilerParams(dimension_semantics=sem, vmem_limit_bytes=VMEM_LIMIT)


def _dot(a, b):
    return jnp.dot(a, b, preferred_element_type=F32)


def _dot_nt(a, b):
    return lax.dot_general(a, b, (((1,), (1,)), ((), ())), preferred_element_type=F32)


def _inproj_kernel(x_ref, w_ref, zf_ref, zb_ref):
    z = _dot(x_ref[...].astype(BF16), w_ref[...])
    zf_ref[...] = z
    zb_ref[...] = z.astype(BF16)


def _inproj(x, w, tm, tn):
    m, k = x.shape
    n = w.shape[1]
    return pl.pallas_call(
        _inproj_kernel,
        grid=(m // tm, n // tn),
        in_specs=[pl.BlockSpec((tm, k), lambda i, j: (i, 0)),
                  pl.BlockSpec((k, tn), lambda i, j: (0, j))],
        out_specs=[pl.BlockSpec((tm, tn), lambda i, j: (i, j)),
                   pl.BlockSpec((tm, tn), lambda i, j: (i, j))],
        out_shape=[jax.ShapeDtypeStruct((m, n), F32), jax.ShapeDtypeStruct((m, n), BF16)],
        compiler_params=_cparams(("parallel", "arbitrary")),
        name="inproj",
    )(x, w)


def _sb_chunk(q, k, v, r_ref, acc_ref, upper, ones, strict):
    z = _dot_nt(q, k) * (HEAD_DIM ** -0.5)
    lk = -(jnp.maximum(z, 0.0) + jnp.log1p(jnp.exp(-jnp.abs(z))))
    if strict is not None:
        lk = jnp.where(strict, lk, 0.0)
    hi = lk.astype(BF16)
    lo = (lk - hi.astype(F32)).astype(BF16)
    after = _dot(hi, upper) + _dot(lo, upper) + r_ref[...]
    tot = _dot(hi, ones) + _dot(lo, ones)
    a = jnp.exp(z + lk + after)
    if strict is not None:
        a = jnp.where(strict, a, 0.0)
    acc_ref[...] += _dot(a.astype(BF16), v)
    r_new = r_ref[...] + tot
    r_ref[...] = r_new
    return jnp.max(r_new[:, :LANES])


def _sb_consts(t):
    row = lax.broadcasted_iota(I32, (t, t), 0)
    col = lax.broadcasted_iota(I32, (t, t), 1)
    upper = jnp.where(row > col, 1.0, 0.0).astype(BF16)
    ones = jnp.ones((t, t), BF16)
    strict = col < row
    return upper, ones, strict


def _sb_prompt_kernel(q_ref, k_ref, v_ref, o_ref, r_ref, acc_ref, *, tq):
    seq = q_ref.shape[0]
    upper, ones, strict = _sb_consts(tq)

    def qblock(i, carry):
        r0 = pl.multiple_of(i * tq, tq)
        q = q_ref[pl.ds(r0, tq), :]
        r_ref[...] = jnp.zeros_like(r_ref)
        acc_ref[...] = jnp.zeros_like(acc_ref)
        rmax = _sb_chunk(q, k_ref[pl.ds(r0, tq), :], v_ref[pl.ds(r0, tq), :],
                         r_ref, acc_ref, upper, ones, strict)

        def cond(st):
            c, rm = st
            return jnp.logical_and(c >= 0, rm > SB_SKIP)

        def body(st):
            c, _ = st
            c0 = pl.multiple_of(c * tq, tq)
            rm = _sb_chunk(q, k_ref[pl.ds(c0, tq), :], v_ref[pl.ds(c0, tq), :],
                           r_ref, acc_ref, upper, ones, None)
            return c - 1, rm

        lax.while_loop(cond, body, (i - 1, rmax))
        o_ref[pl.ds(r0, tq), :] = acc_ref[...].astype(o_ref.dtype)
        return carry

    lax.fori_loop(0, seq // tq, qblock, 0)


def _sb_prompt(zb, n_batch, seq, tq=256):
    qb0, kb0, vb0 = OFF_QB // LANES, OFF_KB // LANES, OFF_VB // LANES
    blk = lambda off: pl.BlockSpec((seq, HEAD_DIM), lambda b, h: (b, off + h))
    return pl.pallas_call(
        functools.partial(_sb_prompt_kernel, tq=tq),
        grid=(n_batch, H_B),
        in_specs=[blk(qb0), blk(kb0), blk(vb0)],
        out_specs=pl.BlockSpec((seq, HEAD_DIM), lambda b, h: (b, h)),
        out_shape=jax.ShapeDtypeStruct((n_batch * seq, H_B * HEAD_DIM), BF16),
        scratch_shapes=[pltpu.VMEM((tq, tq), F32), pltpu.VMEM((tq, HEAD_DIM), F32)],
        compiler_params=_cparams(("parallel", "parallel")),
        name="sb_prompt",
    )(zb, zb, zb)


def _float_key(s):
    bits = lax.bitcast_convert_type(s, I32)
    return bits ^ (lax.shift_right_arithmetic(bits, 31) & 0x7FFFFFFF)


def _indexer_chunk(qi_ref, kd, w, tq, ck):
    kk = jnp.concatenate([kd[:, :LANES], kd[:, LANES:]], axis=0)
    s = jnp.zeros((tq, ck), F32)
    for p in range(H_I // 2):
        s2 = _dot_nt(qi_ref[:, p * LANES:(p + 1) * LANES], kk)
        s = s + jnp.maximum(s2[:, :ck], 0.0) * w[:, 2 * p:2 * p + 1]
        s = s + jnp.maximum(s2[:, ck:], 0.0) * w[:, 2 * p + 1:2 * p + 2]
    return s


def _topk_threshold(keys_ref, n_chunks, topk, tq):
    def bit_step(it, t_u):
        bit = 31 - it
        cand = t_u | lax.shift_left(jnp.int32(1), bit)
        cand_s = cand ^ INT_MIN

        def count(c, acc):
            ge = jnp.where(keys_ref[c] >= cand_s, 1.0, 0.0)
            return acc + ge[:, :LANES] + ge[:, LANES:]

        acc = lax.fori_loop(0, n_chunks, count, jnp.zeros((tq, LANES), F32))
        cnt = jnp.sum(acc, axis=1, keepdims=True)
        return jnp.where(cnt >= topk, cand, t_u)

    t_u = lax.fori_loop(0, 32, bit_step, jnp.zeros((tq, 1), I32))
    return t_u ^ INT_MIN


def _dsa_attend_chunk(qa_ref, kc, vc, sel, bias_fn, m_ref, acc_ref, ck):
    ones = jnp.ones((ck, HEAD_DIM), BF16)
    rep = H_A // KV_A
    for g in range(KV_A):
        kg = kc[:, g * HEAD_DIM:(g + 1) * HEAD_DIM]
        vg = jnp.concatenate([vc[:, g * HEAD_DIM:(g + 1) * HEAD_DIM], ones], axis=1)
        for r in range(rep):
            h = g * rep + r
            lg = _dot_nt(qa_ref[:, h * HEAD_DIM:(h + 1) * HEAD_DIM], kg) * (HEAD_DIM ** -0.5) + bias_fn(h)
            lg = jnp.where(sel, lg, MASKED)
            m_old = m_ref[h]
            m_new = jnp.maximum(m_old, jnp.max(lg, axis=1, keepdims=True))
            alpha = jnp.exp(m_old - m_new)
            p = jnp.exp(lg - jnp.concatenate([m_new] * (ck // LANES), axis=1))
            sl = slice(h * 2 * HEAD_DIM, (h + 1) * 2 * HEAD_DIM)
            acc_ref[:, sl] = acc_ref[:, sl] * jnp.concatenate([alpha, alpha], axis=1) + _dot(p.astype(BF16), vg)
            m_ref[h] = m_new


def _dsa_prompt_kernel(b31_ref, qi_ref, sm_ref, kk_ref, qa_ref, ka_ref, va_ref, t0_ref, t1_ref,
                       o_ref, keys_ref, m_ref, acc_ref, *, tq, topk):
    ck = tq
    i = pl.program_id(1)
    w = sm_ref[:, :H_I] * ((D_I ** -0.5) * (H_I ** -0.5))
    row = lax.broadcasted_iota(I32, (tq, ck), 0)
    col = lax.broadcasted_iota(I32, (tq, ck), 1)
    causal = col <= row

    def score_chunk(c, carry):
        c0 = pl.multiple_of(c * ck, ck)
        s = _indexer_chunk(qi_ref, kk_ref[pl.ds(c0, ck), :], w, tq, ck)
        keys_ref[c] = _float_key(s)
        return carry

    lax.fori_loop(0, i, score_chunk, 0)
    i0 = pl.multiple_of(i * ck, ck)
    s = _indexer_chunk(qi_ref, kk_ref[pl.ds(i0, ck), :], w, tq, ck)
    keys_ref[i] = _float_key(jnp.where(causal, s, -jnp.inf))

    thr = _topk_threshold(keys_ref, i + 1, topk, tq)

    m_ref[...] = jnp.full(m_ref.shape, M_INIT, F32)
    acc_ref[...] = jnp.zeros_like(acc_ref)

    def far_chunk(c, carry):
        c0 = pl.multiple_of(c * ck, ck)
        _dsa_attend_chunk(qa_ref, ka_ref[pl.ds(c0, ck), :], va_ref[pl.ds(c0, ck), :],
                          keys_ref[c] >= thr, lambda h: b31_ref[h], m_ref, acc_ref, ck)
        return carry

    lax.fori_loop(0, i - 1, far_chunk, 0)

    @pl.when(i >= 1)
    def _():
        c = i - 1
        c0 = pl.multiple_of(c * ck, ck)
        _dsa_attend_chunk(qa_ref, ka_ref[pl.ds(c0, ck), :], va_ref[pl.ds(c0, ck), :],
                          keys_ref[c] >= thr, lambda h: t1_ref[h], m_ref, acc_ref, ck)

    _dsa_attend_chunk(qa_ref, ka_ref[pl.ds(i0, ck), :], va_ref[pl.ds(i0, ck), :],
                      jnp.logical_and(keys_ref[i] >= thr, causal), lambda h: t0_ref[h], m_ref, acc_ref, ck)

    for h in range(H_A):
        a = acc_ref[:, h * 2 * HEAD_DIM:(h + 1) * 2 * HEAD_DIM]
        o_ref[:, h * HEAD_DIM:(h + 1) * HEAD_DIM] = (a[:, :HEAD_DIM] / a[:, HEAD_DIM:]).astype(o_ref.dtype)


def _bucket_table(n):
    d = np.arange(n)
    max_exact = NUM_BUCKETS // 2
    nf = np.maximum(d, 1).astype(np.float32)
    large = max_exact + (np.log(nf / np.float32(max_exact)) / np.float32(math.log(MAX_DISTANCE / max_exact))
                         * np.float32(NUM_BUCKETS - max_exact)).astype(np.int32)
    large = np.minimum(large, NUM_BUCKETS - 1)
    return np.where(d < max_exact, d, large).astype(np.int32)


def _bias_tiles(rel_bias, tq):
    bd = rel_bias[_bucket_table(2 * tq)]
    r = np.arange(tq)[:, None]
    j = np.arange(tq)[None, :]
    t0 = jnp.transpose(bd[np.maximum(r - j, 0)], (2, 0, 1))
    t1 = jnp.transpose(bd[tq + r - j], (2, 0, 1))
    return t0, t1, rel_bias[NUM_BUCKETS - 1]


def _dsa_prompt(zf, zb, rel_bias, n_batch, seq, off_ka, off_va, off_sm, tq=256):
    topk = min(TOPK_MAX, seq // 4)
    nq = seq // tq
    t0, t1, b31 = _bias_tiles(rel_bias, tq)
    assert _bucket_table(4 * tq)[tq + 1:].min() == NUM_BUCKETS - 1
    qw = H_I * D_I
    kvw = KV_A * HEAD_DIM
    return pl.pallas_call(
        functools.partial(_dsa_prompt_kernel, tq=tq, topk=topk),
        grid=(n_batch, nq),
        in_specs=[
            pl.BlockSpec(memory_space=pltpu.SMEM),
            pl.BlockSpec((tq, qw), lambda b, i: (b * nq + i, OFF_QI // qw)),
            pl.BlockSpec((tq, LANES), lambda b, i: (b * nq + i, off_sm // LANES + 2)),
            pl.BlockSpec((seq, 2 * LANES), lambda b, i: (b, off_sm // (2 * LANES))),
            pl.BlockSpec((tq, _QA), lambda b, i: (b * nq + i, OFF_QA // _QA)),
            pl.BlockSpec((seq, kvw), lambda b, i: (b, off_ka // kvw)),
            pl.BlockSpec((seq, kvw), lambda b, i: (b, off_va // kvw)),
            pl.BlockSpec((H_A, tq, tq), lambda b, i: (0, 0, 0)),
            pl.BlockSpec((H_A, tq, tq), lambda b, i: (0, 0, 0)),
        ],
        out_specs=pl.BlockSpec((tq, _QA), lambda b, i: (b * nq + i, 0)),
        out_shape=jax.ShapeDtypeStruct((n_batch * seq, _QA), BF16),
        scratch_shapes=[pltpu.VMEM((nq, tq, tq), I32),
                        pltpu.VMEM((H_A, tq, LANES), F32),
                        pltpu.VMEM((tq, H_A * 2 * HEAD_DIM), F32)],
        compiler_params=_cparams(("parallel", "arbitrary")),
        name="dsa_prompt",
    )(b31, zb, zf, zb, zb, zb, zb, t0, t1)


def _merge_kernel(oa_ref, ob_ref, wa_ref, wb_ref, ga_ref, gb_ref, m_ref):
    pa = _dot(oa_ref[...], wa_ref[...].astype(BF16))
    pb = _dot(ob_ref[...], wb_ref[...].astype(BF16))
    m_ref[...] = (jax.nn.sigmoid(ga_ref[...]) * pa + jax.nn.sigmoid(gb_ref[...]) * pb).astype(m_ref.dtype)


def _merge(oa, ob, w_pa, w_pb, zf, off_ga, off_gb, tm, tn=512):
    m, ka = oa.shape
    kb = ob.shape[1]
    n = w_pa.shape[1]
    return pl.pallas_call(
        _merge_kernel,
        grid=(m // tm, n // tn),
        in_specs=[pl.BlockSpec((tm, ka), lambda i, j: (i, 0)),
                  pl.BlockSpec((tm, kb), lambda i, j: (i, 0)),
                  pl.BlockSpec((ka, tn), lambda i, j: (0, j)),
                  pl.BlockSpec((kb, tn), lambda i, j: (0, j)),
                  pl.BlockSpec((tm, tn), lambda i, j: (i, off_ga // tn + j)),
                  pl.BlockSpec((tm, tn), lambda i, j: (i, off_gb // tn + j))],
        out_specs=pl.BlockSpec((tm, tn), lambda i, j: (i, j)),
        out_shape=jax.ShapeDtypeStruct((m, n), BF16),
        compiler_params=_cparams(("parallel", "arbitrary")),
        name="merge",
    )(oa, ob, w_pa, w_pb, zf, zf)


def _mm_ln_kernel(x_ref, a_ref, w_ref, g_ref, b_ref, yf_ref, yb_ref, acc_ref, *, alpha):
    k = pl.program_id(1)

    @pl.when(k == 0)
    def _():
        acc_ref[...] = jnp.zeros_like(acc_ref)

    acc_ref[...] += _dot(a_ref[...].astype(BF16), w_ref[...].astype(BF16))

    @pl.when(k == pl.num_programs(1) - 1)
    def _():
        y = alpha * x_ref[...] + acc_ref[...]
        mu = jnp.mean(y, axis=-1, keepdims=True)
        yc = y - mu
        var = jnp.mean(yc * yc, axis=-1, keepdims=True)
        out = yc * lax.rsqrt(var + LN_EPS) * g_ref[...] + b_ref[...]
        yf_ref[...] = out
        yb_ref[...] = out.astype(BF16)


def _mm_ln(x, a, w, g, b, alpha, tm, tk=512):
    m, n = x.shape
    kdim = a.shape[1]
    return pl.pallas_call(
        functools.partial(_mm_ln_kernel, alpha=alpha),
        grid=(m // tm, kdim // tk),
        in_specs=[pl.BlockSpec((tm, n), lambda i, k: (i, 0)),
                  pl.BlockSpec((tm, tk), lambda i, k: (i, k)),
                  pl.BlockSpec((tk, n), lambda i, k: (k, 0)),
                  pl.BlockSpec((1, n), lambda i, k: (0, 0)),
                  pl.BlockSpec((1, n), lambda i, k: (0, 0))],
        out_specs=[pl.BlockSpec((tm, n), lambda i, k: (i, 0)),
                   pl.BlockSpec((tm, n), lambda i, k: (i, 0))],
        out_shape=[jax.ShapeDtypeStruct((m, n), F32), jax.ShapeDtypeStruct((m, n), BF16)],
        scratch_shapes=[pltpu.VMEM((tm, n), F32)],
        compiler_params=_cparams(("parallel", "arbitrary")),
        name="mm_ln",
    )(x, a, w, g.reshape(1, n), b.reshape(1, n))


def _gelu_tanh(x):
    return 0.5 * x * (1.0 + jnp.tanh(math.sqrt(2.0 / math.pi) * (x + 0.044715 * (x * x * x))))


def _ffn_up_kernel(x_ref, wa_ref, wv_ref, cw_ref, cb_ref, prev_ref, h_ref, tail_ref, ext_ref,
                   *, tm, tiles_per_seq):
    i = pl.program_id(1)
    x = x_ref[...].astype(BF16)
    a = _dot(x, wa_ref[...].astype(BF16))
    val = _dot(x, wv_ref[...].astype(BF16))

    @pl.when(i % tiles_per_seq == 0)
    def _():
        ext_ref[0:SUBLANES, :] = prev_ref[0]

    ext_ref[SUBLANES:SUBLANES + tm, :] = a
    cw = cw_ref[...]
    c = (cb_ref[...]
         + cw[0:1, :] * ext_ref[SUBLANES - 2:SUBLANES - 2 + tm, :]
         + cw[1:2, :] * ext_ref[SUBLANES - 1:SUBLANES - 1 + tm, :]
         + cw[2:3, :] * a)
    h_ref[...] = (_gelu_tanh(c) * val).astype(h_ref.dtype)
    last = ext_ref[tm:tm + SUBLANES, :]
    tail_ref[0] = last
    ext_ref[0:SUBLANES, :] = last


def _ffn_up(x, w_up, conv_w, conv_b, prev8, tm, rows_per_seq, h_dtype, tf=512):
    m, d = x.shape
    f = conv_w.shape[1]
    nf = f // tf
    tiles_per_seq = rows_per_seq // tm
    return pl.pallas_call(
        functools.partial(_ffn_up_kernel, tm=tm, tiles_per_seq=tiles_per_seq),
        grid=(nf, m // tm),
        in_specs=[pl.BlockSpec((tm, d), lambda j, i: (i, 0)),
                  pl.BlockSpec((d, tf), lambda j, i: (0, j)),
                  pl.BlockSpec((d, tf), lambda j, i: (0, nf + j)),
                  pl.BlockSpec((CONV_W, tf), lambda j, i: (0, j)),
                  pl.BlockSpec((1, tf), lambda j, i: (0, j)),
                  pl.BlockSpec((1, SUBLANES, tf), lambda j, i: (i // tiles_per_seq, 0, j))],
        out_specs=[pl.BlockSpec((tm, tf), lambda j, i: (i, j)),
                   pl.BlockSpec((1, SUBLANES, tf), lambda j, i: (i // tiles_per_seq, 0, j))],
        out_shape=[jax.ShapeDtypeStruct((m, f), h_dtype),
                   jax.ShapeDtypeStruct((m // rows_per_seq, SUBLANES, f), F32)],
        scratch_shapes=[pltpu.VMEM((tm + SUBLANES, tf), F32)],
        compiler_params=_cparams(("parallel", "arbitrary")),
        name="ffn_up",
    )(x, w_up, w_up, conv_w, conv_b.reshape(1, f), prev8)


def _t5_bucket(rel):
    n = jnp.maximum(rel, 0)
    max_exact = NUM_BUCKETS // 2
    nf = jnp.maximum(n, 1).astype(F32)
    large = max_exact + (jnp.log(nf / max_exact) / math.log(MAX_DISTANCE / max_exact)
                         * (NUM_BUCKETS - max_exact)).astype(I32)
    large = jnp.minimum(large, NUM_BUCKETS - 1)
    return jnp.where(n < max_exact, n, large)


def _dsa_attend_jax(q, k_sel, v_sel, sel_pos, q_pos, rel_bias):
    n, tq, h, dh = q.shape
    g = k_sel.shape[3]
    r = h // g
    kk = k_sel.shape[2]
    qg = q.reshape(n, tq, g, r, dh)
    logits = jnp.einsum('btgrd,btkgd->btgrk', qg, k_sel).astype(F32) * dh ** -0.5
    rel = q_pos[None, :, None] - sel_pos
    valid = rel >= 0
    bias = rel_bias[_t5_bucket(rel)].astype(F32)
    bias = bias.reshape(n, tq, kk, g, r).transpose(0, 1, 3, 4, 2)
    logits = jnp.where(valid[:, :, None, None, :], logits + bias, -jnp.inf)
    p = jax.nn.softmax(logits, axis=-1)
    o = jnp.einsum('btgrk,btkgd->btgrd', p, v_sel)
    return o.reshape(n, tq, h * dh)


def _dsa_sample_jax(q_a, k_new, v_new, q_i, ki_new, w_i, pool_k, pool_v, pool_ki, page_table, rel_bias):
    n, t = q_a.shape[:2]
    page = pool_k.shape[1]
    past = page_table.shape[1] * page
    total = past + t
    topk = min(TOPK_MAX, total // 4)
    ki_past = pool_ki[page_table].reshape(n, past, D_I)
    ki_all = jnp.concatenate([ki_past, ki_new], axis=1)
    q_pos = past + jnp.arange(t, dtype=I32)
    k_pos = jnp.arange(total, dtype=I32)
    s = jnp.einsum('bthd,bsd->bths', q_i, ki_all).astype(F32) * D_I ** -0.5
    s = jnp.einsum('bths,bth->bts', jax.nn.relu(s), w_i) * H_I ** -0.5
    s = jnp.where(k_pos[None, None, :] <= q_pos[None, :, None], s, -jnp.inf)
    _, idx = lax.top_k(s, topk)
    in_past = idx < past
    pidx = jnp.minimum(idx, past - 1)
    phys = jax.vmap(lambda pt, ii: pt[ii])(page_table, pidx // page)
    off = pidx % page
    nidx = jnp.clip(idx - past, 0, t - 1)
    gather_rows = jax.vmap(lambda rows, ii: rows[ii])
    sel = in_past[..., None, None]
    k_sel = jnp.where(sel, pool_k[phys, off], gather_rows(k_new, nidx))
    v_sel = jnp.where(sel, pool_v[phys, off], gather_rows(v_new, nidx))
    return _dsa_attend_jax(q_a, k_sel, v_sel, idx, q_pos, rel_bias)


def _sb_sample_jax(q, k_new, v_new, pool_k, pool_v, page_table):
    n, t = q.shape[:2]
    past = page_table.shape[1] * pool_k.shape[1]
    k_all = jnp.concatenate([pool_k[page_table].reshape(n, past, H_B, HEAD_DIM), k_new], axis=1)
    v_all = jnp.concatenate([pool_v[page_table].reshape(n, past, H_B, HEAD_DIM), v_new], axis=1)
    q_pos = past + jnp.arange(t, dtype=I32)
    k_pos = jnp.arange(past + t, dtype=I32)
    z = jnp.einsum('bthd,bshd->bhts', q, k_all).astype(F32) * HEAD_DIM ** -0.5
    mask = k_pos[None, :] < q_pos[:, None]
    log_keep_all = jax.nn.log_sigmoid(-z)
    log_beta = z + log_keep_all
    log_keep = jnp.where(mask, log_keep_all, 0.0)
    after = lax.cumsum(log_keep, axis=3, reverse=True) - log_keep
    a = jnp.where(mask, jnp.exp(log_beta + after), 0.0)
    o = jnp.einsum('bhts,bshd->bthd', a, v_all)
    return o.reshape(n, t, H_B * HEAD_DIM)


def _pad_w_in(w_in):
    sizes = (H_A * HEAD_DIM, KV_A * HEAD_DIM, KV_A * HEAD_DIM, H_I * D_I, D_I, H_I,
             H_B * HEAD_DIM, H_B * HEAD_DIM, H_B * HEAD_DIM)
    offs = np.cumsum((0,) + sizes)
    qa, ka, va, qi, ki, wi, qb, kb, vb = [w_in[..., offs[t]:offs[t + 1]] for t in range(9)]
    g = w_in[..., offs[9]:]
    zk = jnp.zeros_like(ki)
    pad = jnp.zeros(w_in.shape[:-1] + (SMALL_W - 4 * D_I - H_I,), w_in.dtype)
    return jnp.concatenate([qa, qi, qb, kb, vb, g, ka, va, ki, zk, zk, ki, wi, pad], axis=-1).astype(BF16)


def kernel(x_prompt, x_sample, cache_k_a, cache_v_a, cache_k_idx, cache_k_b, cache_v_b, state_conv,
           page_table, rel_bias, w_in, w_proj_a, w_proj_b, w_out, ln1_g, ln1_b, w_up, conv_w, conv_b,
           w_down, ln2_g, ln2_b):
    depth = w_in.shape[0]
    n_p, seq, d_model = x_prompt.shape
    n_s, dec = x_sample.shape[:2]
    d_ff = conv_w.shape[-1]
    alpha = (2 * depth) ** 0.25

    off_ga = OFF_VB + _QB
    off_gb = off_ga + d_model
    off_ka = off_gb + d_model
    off_va = off_ka + KV_A * HEAD_DIM
    off_sm = off_va + KV_A * HEAD_DIM
    w_in_p = _pad_w_in(w_in)
    p_pad = w_in_p.shape[-1]
    tn_in = next(t for t in (1024, 512, 256, LANES) if p_pad % t == 0)

    mp = n_p * seq
    ms = n_s * dec
    xp_f = x_prompt.reshape(mp, d_model)
    xp_b = xp_f.astype(BF16)
    xs_f = x_sample.reshape(ms, d_model)
    tm_p = min(1024, seq)
    tm_ln = min(512, seq)
    zero_prev = jnp.zeros((n_p, SUBLANES, d_ff), F32)

    outs = [[] for _ in range(12)]
    for l in range(depth):
        zf, zb = _inproj(xp_b, w_in_p[l], tm_p, tn_in)
        oa = _dsa_prompt(zf, zb, rel_bias, n_p, seq, off_ka, off_va, off_sm)
        ob = _sb_prompt(zb, n_p, seq)
        mg = _merge(oa, ob, w_proj_a[l], w_proj_b[l], zf, off_ga, off_gb, tm_p)
        xp_f, xp_b = _mm_ln(xp_f, mg, w_out[l], ln1_g[l], ln1_b[l], alpha, tm_ln)
        h, tail = _ffn_up(xp_b, w_up[l], conv_w[l], conv_b[l], zero_prev, tm_p, seq, BF16)
        xp_f, xp_b = _mm_ln(xp_f, h, w_down[l], ln2_g[l], ln2_b[l], alpha, tm_ln)
        outs[0].append(zf[:, off_ka:off_ka + KV_A * HEAD_DIM].reshape(n_p, seq, KV_A, HEAD_DIM))
        outs[1].append(zf[:, off_va:off_va + KV_A * HEAD_DIM].reshape(n_p, seq, KV_A, HEAD_DIM))
        outs[2].append(zf[:, off_sm:off_sm + D_I].reshape(n_p, seq, D_I))
        outs[3].append(zf[:, OFF_KB:OFF_KB + _QB].reshape(n_p, seq, H_B, HEAD_DIM))
        outs[4].append(zf[:, OFF_VB:OFF_VB + _QB].reshape(n_p, seq, H_B, HEAD_DIM))
        outs[5].append(tail[:, SUBLANES - (CONV_W - 1):, :])

        zs, _ = _inproj(xs_f, w_in_p[l], ms, tn_in)
        z3 = zs.reshape(n_s, dec, p_pad)
        qa = z3[..., OFF_QA:OFF_QA + _QA].reshape(n_s, dec, H_A, HEAD_DIM)
        qi = z3[..., OFF_QI:OFF_QI + _QI].reshape(n_s, dec, H_I, D_I)
        qb = z3[..., OFF_QB:OFF_QB + _QB].reshape(n_s, dec, H_B, HEAD_DIM)
        kb = z3[..., OFF_KB:OFF_KB + _QB].reshape(n_s, dec, H_B, HEAD_DIM)
        vb = z3[..., OFF_VB:OFF_VB + _QB].reshape(n_s, dec, H_B, HEAD_DIM)
        ka = z3[..., off_ka:off_ka + KV_A * HEAD_DIM].reshape(n_s, dec, KV_A, HEAD_DIM)
        va = z3[..., off_va:off_va + KV_A * HEAD_DIM].reshape(n_s, dec, KV_A, HEAD_DIM)
        ki = z3[..., off_sm:off_sm + D_I]
        wi = z3[..., off_sm + 2 * LANES:off_sm + 2 * LANES + H_I]
        oa_s = _dsa_sample_jax(qa, ka, va, qi, ki, wi, cache_k_a[l], cache_v_a[l], cache_k_idx[l],
                               page_table, rel_bias)
        ob_s = _sb_sample_jax(qb, kb, vb, cache_k_b[l], cache_v_b[l], page_table)
        mg_s = _merge(oa_s.reshape(ms, _QA).astype(BF16), ob_s.reshape(ms, _QB).astype(BF16),
                      w_proj_a[l], w_proj_b[l], zs, off_ga, off_gb, ms)
        xs_f, _ = _mm_ln(xs_f, mg_s, w_out[l], ln1_g[l], ln1_b[l], alpha, ms)
        prev8 = jnp.concatenate([jnp.zeros((n_s, SUBLANES - (CONV_W - 1), d_ff), F32), state_conv[l]], axis=1)
        h_s, tail_s = _ffn_up(xs_f, w_up[l], conv_w[l], conv_b[l], prev8, dec, dec, F32)
        xs_f, _ = _mm_ln(xs_f, h_s, w_down[l], ln2_g[l], ln2_b[l], alpha, ms)
        outs[6].append(ka)
        outs[7].append(va)
        outs[8].append(ki)
        outs[9].append(kb)
        outs[10].append(vb)
        outs[11].append(tail_s[:, SUBLANES - (CONV_W - 1):, :])

    return (xp_f.reshape(n_p, seq, d_model), xs_f.reshape(n_s, dec, d_model)) + tuple(jnp.stack(o) for o in outs)
```

```python
import functools
import math

import numpy as np
import jax
import jax.numpy as jnp
from jax import lax
from jax.experimental import pallas as pl
from jax.experimental.pallas import tpu as pltpu

F32 = jnp.float32
BF16 = jnp.bfloat16
I32 = jnp.int32

HEAD_DIM = 128
H_A = 8
KV_A = 2
H_I = 16
D_I = 64
TOPK_MAX = 256
H_B = 8
NUM_BUCKETS = 32
MAX_DISTANCE = 128
CONV_W = 3
LN_EPS = 1e-5

LANES = 128
SUBLANES = 8
VMEM_LIMIT = 56 * 1024 * 1024

_QA = H_A * HEAD_DIM
_QI = H_I * D_I
_QB = H_B * HEAD_DIM
_KVA = KV_A * HEAD_DIM
OFF_QA = 0
OFF_QI = OFF_QA + _QA
OFF_QB = OFF_QI + _QI
OFF_KB = OFF_QB + _QB
OFF_VB = OFF_KB + _QB
SMALL_W = 4 * LANES
INT_MIN = -2 ** 31
MASKED = -1e30
M_INIT = -1e20
SB_SKIP = -105.0
ATT_SCALE = HEAD_DIM ** -0.5
IDX_SCALE = (D_I ** -0.5) * (H_I ** -0.5)


def _cparams(sem):
    return pltpu.CompilerParams(dimension_semantics=sem, vmem_limit_bytes=VMEM_LIMIT)


def _dot(a, b):
    return jnp.dot(a, b, preferred_element_type=F32)


def _dot_nt(a, b):
    return lax.dot_general(a, b, (((1,), (1,)), ((), ())), preferred_element_type=F32)


def _largest_tile(n, cap):
    return max(t for t in range(LANES, cap + 1, LANES) if n % t == 0)


def _inproj_kernel(x_ref, w_ref, zf_ref, zb_ref):
    z = _dot(x_ref[...].astype(BF16), w_ref[...])
    zf_ref[...] = z
    zb_ref[...] = z.astype(BF16)


def _inproj(x, w, tm, tn):
    m, k = x.shape
    n = w.shape[1]
    return pl.pallas_call(
        _inproj_kernel,
        grid=(m // tm, n // tn),
        in_specs=[pl.BlockSpec((tm, k), lambda i, j: (i, 0)),
                  pl.BlockSpec((k, tn), lambda i, j: (0, j))],
        out_specs=[pl.BlockSpec((tm, tn), lambda i, j: (i, j)),
                   pl.BlockSpec((tm, tn), lambda i, j: (i, j))],
        out_shape=[jax.ShapeDtypeStruct((m, n), F32), jax.ShapeDtypeStruct((m, n), BF16)],
        compiler_params=_cparams(("parallel", "arbitrary")),
        name="inproj",
    )(x, w)


def _sb_chunk(q, k, v, r_ref, acc_ref, upper, ones, mask):
    n = k.shape[0]
    z = _dot_nt(q, k) * ATT_SCALE
    lk = -(jnp.maximum(z, 0.0) + jnp.log1p(jnp.exp(-jnp.abs(z))))
    if mask is not None:
        lk = jnp.where(mask, lk, 0.0)
    hi = lk.astype(BF16)
    lo = (lk - hi.astype(F32)).astype(BF16)
    r = r_ref[...]
    w = r.shape[1]
    rb = r[:, :n] if n <= w else jnp.concatenate([r] * (n // w), axis=1)
    after = _dot(hi, upper) + _dot(lo, upper) + rb
    tot = _dot(hi, ones) + _dot(lo, ones)
    a = jnp.exp(z + lk + after)
    if mask is not None:
        a = jnp.where(mask, a, 0.0)
    acc_ref[...] += _dot(a.astype(BF16), v)
    r_new = r + tot
    r_ref[...] = r_new
    return jnp.max(r_new)


def _sb_prompt_kernel(q_ref, k_ref, v_ref, o_ref, r_ref, acc_ref, *, tq):
    seq = q_ref.shape[0]
    row = lax.broadcasted_iota(I32, (tq, tq), 0)
    col = lax.broadcasted_iota(I32, (tq, tq), 1)
    upper = jnp.where(row > col, 1.0, 0.0).astype(BF16)
    ones = jnp.ones((tq, tq), BF16)
    strict = col < row

    def qblock(i, carry):
        r0 = pl.multiple_of(i * tq, tq)
        q = q_ref[pl.ds(r0, tq), :]
        r_ref[...] = jnp.zeros_like(r_ref)
        acc_ref[...] = jnp.zeros_like(acc_ref)
        rmax = _sb_chunk(q, k_ref[pl.ds(r0, tq), :], v_ref[pl.ds(r0, tq), :],
                         r_ref, acc_ref, upper, ones, strict)

        def cond(st):
            c, rm = st
            return jnp.logical_and(c >= 0, rm > SB_SKIP)

        def body(st):
            c, _ = st
            c0 = pl.multiple_of(c * tq, tq)
            rm = _sb_chunk(q, k_ref[pl.ds(c0, tq), :], v_ref[pl.ds(c0, tq), :],
                           r_ref, acc_ref, upper, ones, None)
            return c - 1, rm

        lax.while_loop(cond, body, (i - 1, rmax))
        o_ref[pl.ds(r0, tq), :] = acc_ref[...].astype(o_ref.dtype)
        return carry

    lax.fori_loop(0, seq // tq, qblock, 0)


def _sb_prompt(zb, n_batch, seq, tq=256):
    qb0, kb0, vb0 = OFF_QB // LANES, OFF_KB // LANES, OFF_VB // LANES
    blk = lambda off: pl.BlockSpec((seq, HEAD_DIM), lambda b, h: (b, off + h))
    return pl.pallas_call(
        functools.partial(_sb_prompt_kernel, tq=tq),
        grid=(n_batch, H_B),
        in_specs=[blk(qb0), blk(kb0), blk(vb0)],
        out_specs=pl.BlockSpec((seq, HEAD_DIM), lambda b, h: (b, h)),
        out_shape=jax.ShapeDtypeStruct((n_batch * seq, H_B * HEAD_DIM), BF16),
        scratch_shapes=[pltpu.VMEM((tq, tq), F32), pltpu.VMEM((tq, HEAD_DIM), F32)],
        compiler_params=_cparams(("parallel", "parallel")),
        name="sb_prompt",
    )(zb, zb, zb)


def _sb_sample_kernel(pt_ref, q_ref, kn_ref, vn_ref, rin_ref, ain_ref, *refs, pg, with_new):
    kp, vp = refs[:pg], refs[pg:2 * pg]
    rout_ref, aout_ref, r_ref, acc_ref = refs[2 * pg:]
    j = pl.program_id(1)
    nq = q_ref.shape[1]
    npg = kp[0].shape[2]
    sh = int(math.log2(H_B))
    q = q_ref[0].astype(BF16)

    def consts(n):
        j2 = lax.shift_right_logical(lax.broadcasted_iota(I32, (n, n), 0), sh)
        j1 = lax.shift_right_logical(lax.broadcasted_iota(I32, (n, n), 1), sh)
        upper = jnp.where(j2 > j1, 1.0, 0.0).astype(BF16)
        row = lax.broadcasted_iota(I32, (nq, n), 0)
        col = lax.broadcasted_iota(I32, (nq, n), 1)
        same_head = (row & (H_B - 1)) == (col & (H_B - 1))
        before = lax.shift_right_logical(col, sh) < lax.shift_right_logical(row, sh)
        return upper, jnp.ones((n, LANES), BF16), same_head, before

    @pl.when(j == 0)
    def _():
        if with_new:
            r_ref[...] = jnp.zeros_like(r_ref)
            acc_ref[...] = jnp.zeros_like(acc_ref)
            upper, ones, same_head, before = consts(nq)
            _sb_chunk(q, kn_ref[0].astype(BF16), vn_ref[0].astype(BF16), r_ref, acc_ref,
                      upper, ones, jnp.logical_and(same_head, before))
        else:
            r_ref[...] = rin_ref[0]
            acc_ref[...] = ain_ref[0]

    upper, ones, same_head, _ = consts(npg)
    for k in range(pg):
        @pl.when(jnp.max(r_ref[...]) > SB_SKIP)
        def _():
            _sb_chunk(q, kp[k][0, 0].astype(BF16), vp[k][0, 0].astype(BF16), r_ref, acc_ref,
                      upper, ones, same_head)

    @pl.when(j == pl.num_programs(1) - 1)
    def _():
        rout_ref[0] = r_ref[...]
        aout_ref[0] = acc_ref[...]


def _sb_sample_call(l, page_table, q, kn, vn, r_in, a_in, pool_k, pool_v, p_hi, n_steps, pg, with_new):
    n_seq, nq, _ = q.shape
    rows = pool_k.shape[2]
    row_blk = pl.BlockSpec((1, nq, HEAD_DIM), lambda b, j, pt: (b, 0, 0))
    page_blk = lambda k: pl.BlockSpec((1, 1, rows, HEAD_DIM),
                                      lambda b, j, pt: (l, pt[b, p_hi - (j * pg + k)], 0, 0))
    grid_spec = pltpu.PrefetchScalarGridSpec(
        num_scalar_prefetch=1,
        grid=(n_seq, n_steps),
        in_specs=[row_blk] * 5 + [page_blk(k) for k in range(pg)] * 2,
        out_specs=[row_blk, row_blk],
        scratch_shapes=[pltpu.VMEM((nq, HEAD_DIM), F32), pltpu.VMEM((nq, HEAD_DIM), F32)],
    )
    return pl.pallas_call(
        functools.partial(_sb_sample_kernel, pg=pg, with_new=with_new),
        grid_spec=grid_spec,
        out_shape=[jax.ShapeDtypeStruct((n_seq, nq, HEAD_DIM), F32)] * 2,
        compiler_params=_cparams(("parallel", "arbitrary")),
        name="sb_sample",
    )(page_table, q, kn, vn, r_in, a_in, *([pool_k] * pg), *([pool_v] * pg))


def _sb_sample(l, page_table, q, kn, vn, pool_k, pool_v, pg_first=4):
    n_pages = page_table.shape[1]
    pg_first = min(pg_first, n_pages)
    zeros = jnp.zeros(q.shape, F32)
    r, acc = _sb_sample_call(l, page_table, q, kn, vn, zeros, zeros, pool_k, pool_v,
                             n_pages - 1, 1, pg_first, True)
    rest = n_pages - pg_first
    if rest > 0:
        pg = max(t for t in (4, 2, 1) if rest % t == 0)

        def tail(r, acc):
            return tuple(_sb_sample_call(l, page_table, q, kn, vn, r, acc, pool_k, pool_v,
                                         rest - 1, rest // pg, pg, False))

        r, acc = lax.cond(jnp.max(r) > SB_SKIP, tail, lambda r, acc: (r, acc), r, acc)
    return acc


def _float_key(s):
    bits = lax.bitcast_convert_type(s, I32)
    return bits ^ (lax.shift_right_arithmetic(bits, 31) & 0x7FFFFFFF)


def _topk_threshold(keys_ref, n_chunks, topk, rows):
    width = keys_ref.shape[2]
    sub = min(rows, 16 * SUBLANES)

    def search(r0):
        def bit_step(it, t_u):
            bit = 31 - it
            cand = t_u | lax.shift_left(jnp.int32(1), bit)
            cand_s = jnp.broadcast_to(cand ^ INT_MIN, (sub, LANES))

            def count(c, acc):
                for t in range(width // LANES):
                    ge = keys_ref[c, r0:r0 + sub, t * LANES:(t + 1) * LANES] >= cand_s
                    acc = acc + jnp.where(ge, 1.0, 0.0)
                return acc

            acc = lax.fori_loop(0, n_chunks, count, jnp.zeros((sub, LANES), F32))
            cnt = jnp.sum(acc, axis=1, keepdims=True)
            return jnp.where(cnt >= topk, cand, t_u)

        return lax.fori_loop(0, 32, bit_step, jnp.zeros((sub, 1), I32))

    t_u = jnp.concatenate([search(r0) for r0 in range(0, rows, sub)], axis=0)
    return t_u ^ INT_MIN


def _softmax_update(lg, v_ext, m_ref, acc_ref, idx, n_sub, sub):
    ck = v_ext.shape[0]
    ps, alphas = [], []
    for i in range(n_sub):
        m_old = m_ref[idx(i)]
        m_new = jnp.maximum(m_old, jnp.max(lg[i], axis=1, keepdims=True))
        alphas.append(jnp.exp(m_old - m_new))
        ps.append(jnp.exp(lg[i] - jnp.concatenate([m_new] * (ck // LANES), axis=1)).astype(BF16))
        m_ref[idx(i)] = m_new
    pv = _dot(jnp.concatenate(ps, axis=0), v_ext)
    for i in range(n_sub):
        a2 = jnp.concatenate([alphas[i], alphas[i]], axis=1)
        acc_ref[idx(i)] = acc_ref[idx(i)] * a2 + pv[i * sub:(i + 1) * sub]


def _dsa_prompt_kernel(b31_ref, qi_ref, sm_ref, kk_ref, qa_ref, ka_ref, va_ref, t0_ref, t1_ref,
                       o_ref, keys_ref, qis_ref, qas_ref, m_ref, acc_ref, *, tq, topk):
    ck = tq
    rep = H_A // KV_A
    i = pl.program_id(1)
    w = sm_ref[:, :H_I] * IDX_SCALE
    row = lax.broadcasted_iota(I32, (tq, ck), 0)
    col = lax.broadcasted_iota(I32, (tq, ck), 1)
    causal = col <= row

    for p in range(H_I // 2):
        qis_ref[p * tq:(p + 1) * tq, :] = qi_ref[:, p * LANES:(p + 1) * LANES]
    for h in range(H_A):
        qas_ref[h * tq:(h + 1) * tq, :] = qa_ref[:, h * HEAD_DIM:(h + 1) * HEAD_DIM]

    def scores(c0):
        kd = kk_ref[pl.ds(c0, ck), :]
        kk = jnp.concatenate([kd[:, :LANES], kd[:, LANES:]], axis=0)
        s2 = _dot_nt(qis_ref[...], kk)
        s = jnp.zeros((tq, ck), F32)
        for p in range(H_I // 2):
            blk = s2[p * tq:(p + 1) * tq]
            s = s + jnp.maximum(blk[:, :ck], 0.0) * w[:, 2 * p:2 * p + 1]
            s = s + jnp.maximum(blk[:, ck:], 0.0) * w[:, 2 * p + 1:2 * p + 2]
        return s

    def score_chunk(c, carry):
        keys_ref[c] = _float_key(scores(pl.multiple_of(c * ck, ck)))
        return carry

    lax.fori_loop(0, i, score_chunk, 0)
    i0 = pl.multiple_of(i * ck, ck)
    keys_ref[i] = _float_key(jnp.where(causal, scores(i0), -jnp.inf))

    thr = _topk_threshold(keys_ref, i + 1, topk, tq)

    m_ref[...] = jnp.full(m_ref.shape, M_INIT, F32)
    acc_ref[...] = jnp.zeros_like(acc_ref)
    ones = jnp.ones((ck, HEAD_DIM), BF16)

    def attend(c0, sel, bias_fn):
        kc = ka_ref[pl.ds(c0, ck), :]
        vc = va_ref[pl.ds(c0, ck), :]
        for g in range(KV_A):
            lg_all = _dot_nt(qas_ref[g * rep * tq:(g + 1) * rep * tq, :],
                             kc[:, g * HEAD_DIM:(g + 1) * HEAD_DIM]) * ATT_SCALE
            lg = [jnp.where(sel, lg_all[r * tq:(r + 1) * tq] + bias_fn(g * rep + r), MASKED)
                  for r in range(rep)]
            v_ext = jnp.concatenate([vc[:, g * HEAD_DIM:(g + 1) * HEAD_DIM], ones], axis=1)
            _softmax_update(lg, v_ext, m_ref, acc_ref, lambda r: g * rep + r, rep, tq)

    def far_chunk(c, carry):
        attend(pl.multiple_of(c * ck, ck), keys_ref[c] >= thr, lambda h: b31_ref[h])
        return carry

    lax.fori_loop(0, i - 1, far_chunk, 0)

    @pl.when(i >= 1)
    def _():
        attend(pl.multiple_of((i - 1) * ck, ck), keys_ref[i - 1] >= thr, lambda h: t1_ref[h])

    attend(i0, jnp.logical_and(keys_ref[i] >= thr, causal), lambda h: t0_ref[h])

    for h in range(H_A):
        a = acc_ref[h]
        o_ref[:, h * HEAD_DIM:(h + 1) * HEAD_DIM] = (a[:, :HEAD_DIM] / a[:, HEAD_DIM:]).astype(o_ref.dtype)


def _bucket_table(n):
    d = np.arange(n)
    max_exact = NUM_BUCKETS // 2
    nf = np.maximum(d, 1).astype(np.float32)
    large = max_exact + (np.log(nf / np.float32(max_exact)) / np.float32(math.log(MAX_DISTANCE / max_exact))
                         * np.float32(NUM_BUCKETS - max_exact)).astype(np.int32)
    large = np.minimum(large, NUM_BUCKETS - 1)
    return np.where(d < max_exact, d, large).astype(np.int32)


def _bias_tiles(rel_bias, tq):
    bd = rel_bias[_bucket_table(2 * tq)]
    r = np.arange(tq)[:, None]
    j = np.arange(tq)[None, :]
    t0 = jnp.transpose(bd[np.maximum(r - j, 0)], (2, 0, 1))
    t1 = jnp.transpose(bd[tq + r - j], (2, 0, 1))
    return t0, t1, rel_bias[NUM_BUCKETS - 1]


def _dsa_prompt(zf, zb, rel_bias, n_batch, seq, off_ka, off_va, off_sm, tq=256):
    topk = min(TOPK_MAX, seq // 4)
    nq = seq // tq
    t0, t1, b31 = _bias_tiles(rel_bias, tq)
    assert _bucket_table(4 * tq)[tq + 1:].min() == NUM_BUCKETS - 1
    return pl.pallas_call(
        functools.partial(_dsa_prompt_kernel, tq=tq, topk=topk),
        grid=(n_batch, nq),
        in_specs=[
            pl.BlockSpec(memory_space=pltpu.SMEM),
            pl.BlockSpec((tq, _QI), lambda b, i: (b * nq + i, OFF_QI // _QI)),
            pl.BlockSpec((tq, LANES), lambda b, i: (b * nq + i, off_sm // LANES + 2)),
            pl.BlockSpec((seq, 2 * LANES), lambda b, i: (b, off_sm // (2 * LANES))),
            pl.BlockSpec((tq, _QA), lambda b, i: (b * nq + i, OFF_QA // _QA)),
            pl.BlockSpec((seq, _KVA), lambda b, i: (b, off_ka // _KVA)),
            pl.BlockSpec((seq, _KVA), lambda b, i: (b, off_va // _KVA)),
            pl.BlockSpec((H_A, tq, tq), lambda b, i: (0, 0, 0)),
            pl.BlockSpec((H_A, tq, tq), lambda b, i: (0, 0, 0)),
        ],
        out_specs=pl.BlockSpec((tq, _QA), lambda b, i: (b * nq + i, 0)),
        out_shape=jax.ShapeDtypeStruct((n_batch * seq, _QA), BF16),
        scratch_shapes=[pltpu.VMEM((nq, tq, tq), I32),
                        pltpu.VMEM((H_I // 2 * tq, LANES), BF16),
                        pltpu.VMEM((H_A * tq, HEAD_DIM), BF16),
                        pltpu.VMEM((H_A, tq, LANES), F32),
                        pltpu.VMEM((H_A, tq, 2 * HEAD_DIM), F32)],
        compiler_params=_cparams(("parallel", "arbitrary")),
        name="dsa_prompt",
    )(b31, zb, zf, zb, zb, zb, zb, t0, t1)


def _idx_sample_kernel(pt_ref, q_ref, w_ref, kin_ref, *refs, pg):
    pages = refs[:pg]
    s_ref, snew_ref = refs[pg:]
    dec = s_ref.shape[2]
    q = q_ref[0]
    w = w_ref[0] * IDX_SCALE

    def score(k):
        s2 = _dot_nt(q, k.astype(BF16))
        s = jnp.zeros((dec, s2.shape[1]), F32)
        for h in range(H_I):
            s = s + jnp.maximum(s2[h * dec:(h + 1) * dec], 0.0) * w[h * dec:(h + 1) * dec]
        return s

    for k in range(pg):
        s_ref[0, k] = score(pages[k][0, 0])

    @pl.when(pl.program_id(1) == 0)
    def _():
        row = lax.broadcasted_iota(I32, snew_ref.shape[1:], 0)
        col = lax.broadcasted_iota(I32, snew_ref.shape[1:], 1)
        snew_ref[0] = jnp.where(col <= row, score(kin_ref[0]), -jnp.inf)


def _idx_sample(l, page_table, q, w, kin, pool_ki, pg):
    n_seq, n_pages = page_table.shape
    dec = q.shape[1] // H_I
    page = pool_ki.shape[2]
    full = lambda shp: pl.BlockSpec((1,) + shp, lambda b, j, pt: (b, 0, 0))
    page_blk = lambda k: pl.BlockSpec((1, 1, page, D_I), lambda b, j, pt: (l, pt[b, j * pg + k], 0, 0))
    grid_spec = pltpu.PrefetchScalarGridSpec(
        num_scalar_prefetch=1,
        grid=(n_seq, n_pages // pg),
        in_specs=[full(q.shape[1:]), full(w.shape[1:]), full(kin.shape[1:])] + [page_blk(k) for k in range(pg)],
        out_specs=[pl.BlockSpec((1, pg, dec, page), lambda b, j, pt: (b, j, 0, 0)),
                   pl.BlockSpec((1, dec, page), lambda b, j, pt: (b, 0, 0))],
    )
    return pl.pallas_call(
        functools.partial(_idx_sample_kernel, pg=pg),
        grid_spec=grid_spec,
        out_shape=[jax.ShapeDtypeStruct((n_seq, n_pages, dec, page), F32),
                   jax.ShapeDtypeStruct((n_seq, dec, page), F32)],
        compiler_params=_cparams(("parallel", "arbitrary")),
        name="idx_sample",
    )(page_table, q, w, kin, *([pool_ki] * pg))


def _dsa_sample_kernel(pt_ref, s_ref, snew_ref, qa_ref, kn_ref, vn_ref, bcol_ref, bpage_ref, bnew_ref,
                       *refs, pg, topk):
    kp, vp = refs[:pg], refs[pg:2 * pg]
    o_ref, keys_ref, thr_ref, m_ref, acc_ref = refs[2 * pg:]
    j = pl.program_id(1)
    last = pl.num_programs(1) - 1
    n_pages = s_ref.shape[1]
    dec, page = s_ref.shape[2], s_ref.shape[3]
    ones = jnp.ones((page, HEAD_DIM), BF16)

    @pl.when(j == 0)
    def _():
        def to_keys(p, carry):
            keys_ref[p] = _float_key(s_ref[0, p])
            return carry

        lax.fori_loop(0, n_pages, to_keys, 0)
        keys_ref[n_pages] = _float_key(snew_ref[0])
        thr = _topk_threshold(keys_ref, n_pages + 1, topk, dec)
        thr_ref[...] = jnp.broadcast_to(thr, thr_ref.shape)
        m_ref[...] = jnp.full(m_ref.shape, M_INIT, F32)
        acc_ref[...] = jnp.zeros_like(acc_ref)

    def tile_rows_cols(sb):
        sb = jnp.concatenate([sb] * (H_A), axis=0)
        return jnp.concatenate([sb] * KV_A, axis=1)

    def fold(kpage, vpage, bias):
        k_cat = jnp.concatenate([kpage[:, g * HEAD_DIM:(g + 1) * HEAD_DIM] for g in range(KV_A)], axis=0)
        v_cat = jnp.concatenate(
            [jnp.concatenate([vpage[:, g * HEAD_DIM:(g + 1) * HEAD_DIM], ones], axis=1) for g in range(KV_A)],
            axis=0)
        lg = _dot_nt(qa_ref[0], k_cat) * ATT_SCALE + bias
        _softmax_update([lg], v_cat, m_ref, acc_ref, lambda _: slice(None), 1, qa_ref.shape[1])

    for k in range(pg):
        sb = tile_rows_cols(jnp.where(keys_ref[j * pg + k] >= thr_ref[...], 0.0, MASKED))
        if k == pg - 1:
            bias = jnp.where(j == last, bpage_ref[...], bcol_ref[...])
        else:
            bias = bcol_ref[...]
        fold(kp[k][0, 0].astype(BF16), vp[k][0, 0].astype(BF16), bias + sb)

    @pl.when(j == last)
    def _():
        row = lax.broadcasted_iota(I32, (dec, page), 0)
        col = lax.broadcasted_iota(I32, (dec, page), 1)
        ok = jnp.logical_and(keys_ref[n_pages] >= thr_ref[...], col <= row)
        sb = tile_rows_cols(jnp.where(ok, 0.0, MASKED))
        fold(kn_ref[0].astype(BF16), vn_ref[0].astype(BF16), bnew_ref[...] + sb)
        a = acc_ref[...]
        o_ref[0] = a[:, :HEAD_DIM] / a[:, HEAD_DIM:]


def _dsa_sample(l, page_table, s, snew, qa, kn, vn, biases, pool_k, pool_v, pg):
    n_seq, n_pages = page_table.shape
    dec, page = s.shape[2], s.shape[3]
    rows = qa.shape[1]
    topk = min(TOPK_MAX, (n_pages * page + dec) // 4)
    full2 = lambda a: pl.BlockSpec(a.shape, lambda b, j, pt: (0, 0))
    page_blk = lambda k: pl.BlockSpec((1, 1, page, _KVA), lambda b, j, pt: (l, pt[b, j * pg + k], 0, 0))
    grid_spec = pltpu.PrefetchScalarGridSpec(
        num_scalar_prefetch=1,
        grid=(n_seq, n_pages // pg),
        in_specs=[pl.BlockSpec((1, n_pages, dec, page), lambda b, j, pt: (b, 0, 0, 0)),
                  pl.BlockSpec((1, dec, page), lambda b, j, pt: (b, 0, 0)),
                  pl.BlockSpec((1, rows, HEAD_DIM), lambda b, j, pt: (b, 0, 0)),
                  pl.BlockSpec((1, page, _KVA), lambda b, j, pt: (b, 0, 0)),
                  pl.BlockSpec((1, page, _KVA), lambda b, j, pt: (b, 0, 0))]
                 + [full2(a) for a in biases] + [page_blk(k) for k in range(pg)] * 2,
        out_specs=pl.BlockSpec((1, rows, HEAD_DIM), lambda b, j, pt: (b, 0, 0)),
        scratch_shapes=[pltpu.VMEM((n_pages + 1, dec, page), I32),
                        pltpu.VMEM((dec, page), I32),
                        pltpu.VMEM((rows, LANES), F32),
                        pltpu.VMEM((rows, 2 * HEAD_DIM), F32)],
    )
    return pl.pallas_call(
        functools.partial(_dsa_sample_kernel, pg=pg, topk=topk),
        grid_spec=grid_spec,
        out_shape=jax.ShapeDtypeStruct((n_seq, rows, HEAD_DIM), F32),
        compiler_params=_cparams(("parallel", "arbitrary")),
        name="dsa_sample",
    )(page_table, s, snew, qa, kn, vn, *biases, *([pool_k] * pg), *([pool_v] * pg))


def _sample_bias(rel_bias, dec, page):
    bd = rel_bias[_bucket_table(2 * page + dec)]
    assert _bucket_table(4 * page)[page + 1:].min() == NUM_BUCKETS - 1
    t = np.arange(dec)[:, None]
    jc = np.arange(page)[None, :]
    rep = H_A // KV_A
    row_g = np.arange(H_A * dec)[:, None] // (rep * dec)
    col_g = np.arange(KV_A * page)[None, :] // page
    other = jnp.asarray(np.where(row_g == col_g, 0.0, MASKED), F32)

    def expand(b):
        b = b.reshape(H_A * dec, page)
        return jnp.concatenate([b] * KV_A, axis=1) + other

    bpage = expand(jnp.transpose(bd[page + t - jc], (2, 0, 1)))
    bnew = expand(jnp.transpose(bd[np.maximum(t - jc, 0)], (2, 0, 1)))
    bcol = expand(jnp.broadcast_to(rel_bias[NUM_BUCKETS - 1][:, None, None], (H_A, dec, page)))
    return bcol, bpage, bnew


def _merge_kernel(oa_ref, ob_ref, wa_ref, wb_ref, ga_ref, gb_ref, m_ref):
    pa = _dot(oa_ref[...], wa_ref[...].astype(BF16))
    pb = _dot(ob_ref[...], wb_ref[...].astype(BF16))
    m_ref[...] = (jax.nn.sigmoid(ga_ref[...]) * pa + jax.nn.sigmoid(gb_ref[...]) * pb).astype(m_ref.dtype)


def _merge(oa, ob, w_pa, w_pb, zf, off_ga, off_gb, tm, tn=512):
    m, ka = oa.shape
    kb = ob.shape[1]
    n = w_pa.shape[1]
    return pl.pallas_call(
        _merge_kernel,
        grid=(m // tm, n // tn),
        in_specs=[pl.BlockSpec((tm, ka), lambda i, j: (i, 0)),
                  pl.BlockSpec((tm, kb), lambda i, j: (i, 0)),
                  pl.BlockSpec((ka, tn), lambda i, j: (0, j)),
                  pl.BlockSpec((kb, tn), lambda i, j: (0, j)),
                  pl.BlockSpec((tm, tn), lambda i, j: (i, off_ga // tn + j)),
                  pl.BlockSpec((tm, tn), lambda i, j: (i, off_gb // tn + j))],
        out_specs=pl.BlockSpec((tm, tn), lambda i, j: (i, j)),
        out_shape=jax.ShapeDtypeStruct((m, n), BF16),
        compiler_params=_cparams(("parallel", "arbitrary")),
        name="merge",
    )(oa, ob, w_pa, w_pb, zf, zf)


def _mm_ln_kernel(x_ref, a_ref, w_ref, g_ref, b_ref, yf_ref, yb_ref, acc_ref, *, alpha):
    k = pl.program_id(1)

    @pl.when(k == 0)
    def _():
        acc_ref[...] = jnp.zeros_like(acc_ref)

    acc_ref[...] += _dot(a_ref[...].astype(BF16), w_ref[...])

    @pl.when(k == pl.num_programs(1) - 1)
    def _():
        y = alpha * x_ref[...] + acc_ref[...]
        mu = jnp.mean(y, axis=-1, keepdims=True)
        yc = y - mu
        var = jnp.mean(yc * yc, axis=-1, keepdims=True)
        out = yc * lax.rsqrt(var + LN_EPS) * g_ref[...] + b_ref[...]
        yf_ref[...] = out
        yb_ref[...] = out.astype(BF16)


def _mm_ln(x, a, w, g, b, alpha, tm):
    m, n = x.shape
    kdim = a.shape[1]
    tk = _largest_tile(kdim, 1408)
    return pl.pallas_call(
        functools.partial(_mm_ln_kernel, alpha=alpha),
        grid=(m // tm, kdim // tk),
        in_specs=[pl.BlockSpec((tm, n), lambda i, k: (i, 0)),
                  pl.BlockSpec((tm, tk), lambda i, k: (i, k)),
                  pl.BlockSpec((tk, n), lambda i, k: (k, 0)),
                  pl.BlockSpec((1, n), lambda i, k: (0, 0)),
                  pl.BlockSpec((1, n), lambda i, k: (0, 0))],
        out_specs=[pl.BlockSpec((tm, n), lambda i, k: (i, 0)),
                   pl.BlockSpec((tm, n), lambda i, k: (i, 0))],
        out_shape=[jax.ShapeDtypeStruct((m, n), F32), jax.ShapeDtypeStruct((m, n), BF16)],
        scratch_shapes=[pltpu.VMEM((tm, n), F32)],
        compiler_params=_cparams(("parallel", "arbitrary")),
        name="mm_ln",
    )(x, a, w, g.reshape(1, n), b.reshape(1, n))


def _gelu_tanh(x):
    return 0.5 * x * (1.0 + jnp.tanh(math.sqrt(2.0 / math.pi) * (x + 0.044715 * (x * x * x))))


def _ffn_up_kernel(x_ref, wa_ref, wv_ref, cw_ref, cb_ref, prev_ref, h_ref, tail_ref, ext_ref,
                   *, tm, tiles_per_seq):
    i = pl.program_id(1)
    x = x_ref[...].astype(BF16)
    a = _dot(x, wa_ref[...].astype(BF16))
    val = _dot(x, wv_ref[...].astype(BF16))

    @pl.when(i % tiles_per_seq == 0)
    def _():
        ext_ref[0:SUBLANES, :] = prev_ref[0]

    ext_ref[SUBLANES:SUBLANES + tm, :] = a
    cw = cw_ref[...]
    c = (cb_ref[...]
         + cw[0:1, :] * ext_ref[SUBLANES - 2:SUBLANES - 2 + tm, :]
         + cw[1:2, :] * ext_ref[SUBLANES - 1:SUBLANES - 1 + tm, :]
         + cw[2:3, :] * a)
    h_ref[...] = (_gelu_tanh(c) * val).astype(h_ref.dtype)
    last = ext_ref[tm:tm + SUBLANES, :]
    tail_ref[0] = last
    ext_ref[0:SUBLANES, :] = last


def _ffn_up(x, w_up, conv_w, conv_b, prev8, tm, rows_per_seq, h_dtype, tf=512):
    m, d = x.shape
    f = conv_w.shape[1]
    nf = f // tf
    tiles_per_seq = rows_per_seq // tm
    return pl.pallas_call(
        functools.partial(_ffn_up_kernel, tm=tm, tiles_per_seq=tiles_per_seq),
        grid=(nf, m // tm),
        in_specs=[pl.BlockSpec((tm, d), lambda j, i: (i, 0)),
                  pl.BlockSpec((d, tf), lambda j, i: (0, j)),
                  pl.BlockSpec((d, tf), lambda j, i: (0, nf + j)),
                  pl.BlockSpec((CONV_W, tf), lambda j, i: (0, j)),
                  pl.BlockSpec((1, tf), lambda j, i: (0, j)),
                  pl.BlockSpec((1, SUBLANES, tf), lambda j, i: (i // tiles_per_seq, 0, j))],
        out_specs=[pl.BlockSpec((tm, tf), lambda j, i: (i, j)),
                   pl.BlockSpec((1, SUBLANES, tf), lambda j, i: (i // tiles_per_seq, 0, j))],
        out_shape=[jax.ShapeDtypeStruct((m, f), h_dtype),
                   jax.ShapeDtypeStruct((m // rows_per_seq, SUBLANES, f), F32)],
        scratch_shapes=[pltpu.VMEM((tm + SUBLANES, tf), F32)],
        compiler_params=_cparams(("parallel", "arbitrary")),
        name="ffn_up",
    )(x, w_up, w_up, conv_w, conv_b.reshape(1, f), prev8)


def _pad_w_in(w_in):
    sizes = (_QA, _KVA, _KVA, _QI, D_I, H_I, _QB, _QB, _QB)
    offs = np.cumsum((0,) + sizes)
    qa, ka, va, qi, ki, wi, qb, kb, vb = [w_in[..., offs[t]:offs[t + 1]] for t in range(9)]
    g = w_in[..., offs[9]:]
    zk = jnp.zeros_like(ki)
    pad = jnp.zeros(w_in.shape[:-1] + (SMALL_W - 4 * D_I - H_I,), w_in.dtype)
    return jnp.concatenate([qa, qi, qb, kb, vb, g, ka, va, ki, zk, zk, ki, wi, pad], axis=-1).astype(BF16)


def _pad_rows(a, rows):
    return jnp.pad(a, ((0, 0), (0, rows - a.shape[1]), (0, 0)))


def kernel(x_prompt, x_sample, cache_k_a, cache_v_a, cache_k_idx, cache_k_b, cache_v_b, state_conv,
           page_table, rel_bias, w_in, w_proj_a, w_proj_b, w_out, ln1_g, ln1_b, w_up, conv_w, conv_b,
           w_down, ln2_g, ln2_b):
    depth = w_in.shape[0]
    n_p, seq, d_model = x_prompt.shape
    n_s, dec = x_sample.shape[:2]
    d_ff = conv_w.shape[-1]
    n_pool, page = cache_k_a.shape[1:3]
    n_pages = page_table.shape[1]
    rep = H_A // KV_A
    alpha = (2 * depth) ** 0.25
    assert dec == SUBLANES and page == LANES

    off_ga = OFF_VB + _QB
    off_gb = off_ga + d_model
    off_ka = off_gb + d_model
    off_va = off_ka + _KVA
    off_sm = off_va + _KVA
    off_wi = off_sm + 2 * LANES
    w_in_p = _pad_w_in(w_in)
    p_pad = w_in_p.shape[-1]
    tn_in = _largest_tile(p_pad, 1024)
    w_out_b = w_out.astype(BF16)
    w_down_b = w_down.astype(BF16)

    pool_ka = cache_k_a.reshape(depth, n_pool, page, _KVA)
    pool_va = cache_v_a.reshape(depth, n_pool, page, _KVA)
    pool_kb = cache_k_b.reshape(depth, n_pool, page * H_B, HEAD_DIM)
    pool_vb = cache_v_b.reshape(depth, n_pool, page * H_B, HEAD_DIM)
    pg_a = max(t for t in (16, 8, 4, 2, 1) if n_pages % t == 0)
    sample_bias = _sample_bias(rel_bias, dec, page)

    mp = n_p * seq
    ms = n_s * dec
    xp_f = x_prompt.reshape(mp, d_model)
    xp_b = xp_f.astype(BF16)
    xs_f = x_sample.reshape(ms, d_model)
    tm_p = min(1024, seq)
    tm_ln = min(512, seq)
    zero_prev = jnp.zeros((n_p, SUBLANES, d_ff), F32)

    outs = [[] for _ in range(12)]
    for l in range(depth):
        zf, zb = _inproj(xp_b, w_in_p[l], tm_p, tn_in)
        oa = _dsa_prompt(zf, zb, rel_bias, n_p, seq, off_ka, off_va, off_sm)
        ob = _sb_prompt(zb, n_p, seq)
        mg = _merge(oa, ob, w_proj_a[l], w_proj_b[l], zf, off_ga, off_gb, tm_p)
        xp_f, xp_b = _mm_ln(xp_f, mg, w_out_b[l], ln1_g[l], ln1_b[l], alpha, tm_ln)
        h, tail = _ffn_up(xp_b, w_up[l], conv_w[l], conv_b[l], zero_prev, tm_p, seq, BF16)
        xp_f, xp_b = _mm_ln(xp_f, h, w_down_b[l], ln2_g[l], ln2_b[l], alpha, tm_ln)
        outs[0].append(zf[:, off_ka:off_ka + _KVA].reshape(n_p, seq, KV_A, HEAD_DIM))
        outs[1].append(zf[:, off_va:off_va + _KVA].reshape(n_p, seq, KV_A, HEAD_DIM))
        outs[2].append(zf[:, off_sm:off_sm + D_I].reshape(n_p, seq, D_I))
        outs[3].append(zf[:, OFF_KB:OFF_KB + _QB].reshape(n_p, seq, H_B, HEAD_DIM))
        outs[4].append(zf[:, OFF_VB:OFF_VB + _QB].reshape(n_p, seq, H_B, HEAD_DIM))
        outs[5].append(tail[:, SUBLANES - (CONV_W - 1):, :])

        zs, _ = _inproj(xs_f, w_in_p[l], ms, tn_in)
        z3 = zs.reshape(n_s, dec, p_pad)
        ka = z3[..., off_ka:off_ka + _KVA]
        va = z3[..., off_va:off_va + _KVA]
        ki = z3[..., off_sm:off_sm + D_I]
        kb = z3[..., OFF_KB:OFF_KB + _QB]
        vb = z3[..., OFF_VB:OFF_VB + _QB]
        qi_s = (z3[..., OFF_QI:OFF_QI + _QI].reshape(n_s, dec, H_I, D_I).transpose(0, 2, 1, 3)
                .reshape(n_s, H_I * dec, D_I).astype(BF16))
        wi_s = z3[..., off_wi:off_wi + H_I].transpose(0, 2, 1).reshape(n_s, H_I * dec, 1)
        qa_s = (z3[..., OFF_QA:OFF_QA + _QA].reshape(n_s, dec, KV_A, rep, HEAD_DIM).transpose(0, 2, 3, 1, 4)
                .reshape(n_s, H_A * dec, HEAD_DIM).astype(BF16))
        s_past, s_new = _idx_sample(l, page_table, qi_s, wi_s, _pad_rows(ki, page), cache_k_idx, pg_a)
        oa_s = _dsa_sample(l, page_table, s_past, s_new, qa_s, _pad_rows(ka, page), _pad_rows(va, page),
                           sample_bias, pool_ka, pool_va, pg_a)
        oa_s = (oa_s.reshape(n_s, KV_A, rep, dec, HEAD_DIM).transpose(0, 3, 1, 2, 4)
                .reshape(ms, _QA).astype(BF16))
        ob_s = _sb_sample(l, page_table, z3[..., OFF_QB:OFF_QB + _QB].reshape(n_s, dec * H_B, HEAD_DIM),
                          kb.reshape(n_s, dec * H_B, HEAD_DIM), vb.reshape(n_s, dec * H_B, HEAD_DIM),
                          pool_kb, pool_vb)
        ob_s = ob_s.reshape(ms, _QB).astype(BF16)
        mg_s = _merge(oa_s, ob_s, w_proj_a[l], w_proj_b[l], zs, off_ga, off_gb, ms)
        xs_f, _ = _mm_ln(xs_f, mg_s, w_out_b[l], ln1_g[l], ln1_b[l], alpha, ms)
        prev8 = jnp.concatenate([jnp.zeros((n_s, SUBLANES - (CONV_W - 1), d_ff), F32), state_conv[l]], axis=1)
        h_s, tail_s = _ffn_up(xs_f, w_up[l], conv_w[l], conv_b[l], prev8, dec, dec, F32)
        xs_f, _ = _mm_ln(xs_f, h_s, w_down_b[l], ln2_g[l], ln2_b[l], alpha, ms)
        outs[6].append(ka.reshape(n_s, dec, KV_A, HEAD_DIM))
        outs[7].append(va.reshape(n_s, dec, KV_A, HEAD_DIM))
        outs[8].append(ki)
        outs[9].append(kb.reshape(n_s, dec, H_B, HEAD_DIM))
        outs[10].append(vb.reshape(n_s, dec, H_B, HEAD_DIM))
        outs[11].append(tail_s[:, SUBLANES - (CONV_W - 1):, :])

    return (xp_f.reshape(n_p, seq, d_model), xs_f.reshape(n_s, dec, d_model)) + tuple(jnp.stack(o) for o in outs)
```

```python
import functools
import math

import numpy as np
import jax
import jax.numpy as jnp
from jax import lax
from jax.experimental import pallas as pl
from jax.experimental.pallas import tpu as pltpu

F32 = jnp.float32
BF16 = jnp.bfloat16
I32 = jnp.int32

HEAD_DIM = 128
H_A = 8
KV_A = 2
H_I = 16
D_I = 64
TOPK_MAX = 256
H_B = 8
NUM_BUCKETS = 32
MAX_DISTANCE = 128
CONV_W = 3
LN_EPS = 1e-5

LANES = 128
SUBLANES = 8
VMEM_LIMIT = 56 * 1024 * 1024

_QA = H_A * HEAD_DIM
_QI = H_I * D_I
_QB = H_B * HEAD_DIM
_KVA = KV_A * HEAD_DIM
OFF_QA = 0
OFF_QI = OFF_QA + _QA
OFF_QB = OFF_QI + _QI
OFF_KB = OFF_QB + _QB
OFF_VB = OFF_KB + _QB
SMALL_W = 4 * LANES
INT_MIN = -2 ** 31
MASKED = -1e30
M_INIT = -1e20
SB_SKIP = -105.0
ATT_SCALE = HEAD_DIM ** -0.5
IDX_SCALE = (D_I ** -0.5) * (H_I ** -0.5)


def _cparams(sem):
    return pltpu.CompilerParams(dimension_semantics=sem, vmem_limit_bytes=VMEM_LIMIT)


def _dot(a, b):
    return jnp.dot(a, b, preferred_element_type=F32)


def _dot_nt(a, b):
    return lax.dot_general(a, b, (((1,), (1,)), ((), ())), preferred_element_type=F32)


def _largest_tile(n, cap):
    return max(t for t in range(LANES, cap + 1, LANES) if n % t == 0)


def _inproj_kernel(x_ref, w_ref, zf_ref, zb_ref):
    z = _dot(x_ref[...].astype(BF16), w_ref[...])
    zf_ref[...] = z
    zb_ref[...] = z.astype(BF16)


def _inproj(x, w, tm, tn):
    m, k = x.shape
    n = w.shape[1]
    return pl.pallas_call(
        _inproj_kernel,
        grid=(m // tm, n // tn),
        in_specs=[pl.BlockSpec((tm, k), lambda i, j: (i, 0)),
                  pl.BlockSpec((k, tn), lambda i, j: (0, j))],
        out_specs=[pl.BlockSpec((tm, tn), lambda i, j: (i, j)),
                   pl.BlockSpec((tm, tn), lambda i, j: (i, j))],
        out_shape=[jax.ShapeDtypeStruct((m, n), F32), jax.ShapeDtypeStruct((m, n), BF16)],
        compiler_params=_cparams(("parallel", "arbitrary")),
        name="inproj",
    )(x, w)


def _sb_chunk(q, k, v, r_ref, acc_ref, upper, ones, mask):
    n = k.shape[0]
    z = _dot_nt(q, k) * ATT_SCALE
    lk = -(jnp.maximum(z, 0.0) + jnp.log1p(jnp.exp(-jnp.abs(z))))
    if mask is not None:
        lk = jnp.where(mask, lk, 0.0)
    hi = lk.astype(BF16)
    lo = (lk - hi.astype(F32)).astype(BF16)
    r = r_ref[...]
    w = r.shape[1]
    rb = r[:, :n] if n <= w else jnp.concatenate([r] * (n // w), axis=1)
    after = _dot(hi, upper) + _dot(lo, upper) + rb
    tot = _dot(hi, ones) + _dot(lo, ones)
    a = jnp.exp(z + lk + after)
    if mask is not None:
        a = jnp.where(mask, a, 0.0)
    acc_ref[...] += _dot(a.astype(BF16), v)
    r_new = r + tot
    r_ref[...] = r_new
    return jnp.max(r_new)


def _sb_prompt_kernel(q_ref, k_ref, v_ref, o_ref, r_ref, acc_ref, *, tq):
    seq = q_ref.shape[0]
    row = lax.broadcasted_iota(I32, (tq, tq), 0)
    col = lax.broadcasted_iota(I32, (tq, tq), 1)
    upper = jnp.where(row > col, 1.0, 0.0).astype(BF16)
    ones = jnp.ones((tq, tq), BF16)
    strict = col < row

    def qblock(i, carry):
        r0 = pl.multiple_of(i * tq, tq)
        q = q_ref[pl.ds(r0, tq), :]
        r_ref[...] = jnp.zeros_like(r_ref)
        acc_ref[...] = jnp.zeros_like(acc_ref)
        rmax = _sb_chunk(q, k_ref[pl.ds(r0, tq), :], v_ref[pl.ds(r0, tq), :],
                         r_ref, acc_ref, upper, ones, strict)

        def cond(st):
            c, rm = st
            return jnp.logical_and(c >= 0, rm > SB_SKIP)

        def body(st):
            c, _ = st
            c0 = pl.multiple_of(c * tq, tq)
            rm = _sb_chunk(q, k_ref[pl.ds(c0, tq), :], v_ref[pl.ds(c0, tq), :],
                           r_ref, acc_ref, upper, ones, None)
            return c - 1, rm

        lax.while_loop(cond, body, (i - 1, rmax))
        o_ref[pl.ds(r0, tq), :] = acc_ref[...].astype(o_ref.dtype)
        return carry

    lax.fori_loop(0, seq // tq, qblock, 0)


def _sb_prompt(zb, n_batch, seq, tq=256):
    qb0, kb0, vb0 = OFF_QB // LANES, OFF_KB // LANES, OFF_VB // LANES
    blk = lambda off: pl.BlockSpec((seq, HEAD_DIM), lambda b, h: (b, off + h))
    return pl.pallas_call(
        functools.partial(_sb_prompt_kernel, tq=tq),
        grid=(n_batch, H_B),
        in_specs=[blk(qb0), blk(kb0), blk(vb0)],
        out_specs=pl.BlockSpec((seq, HEAD_DIM), lambda b, h: (b, h)),
        out_shape=jax.ShapeDtypeStruct((n_batch * seq, H_B * HEAD_DIM), BF16),
        scratch_shapes=[pltpu.VMEM((tq, tq), F32), pltpu.VMEM((tq, HEAD_DIM), F32)],
        compiler_params=_cparams(("parallel", "parallel")),
        name="sb_prompt",
    )(zb, zb, zb)


def _sb_sample_kernel(pt_ref, q_ref, kn_ref, vn_ref, rin_ref, ain_ref, *refs, pg, with_new):
    kp, vp = refs[:pg], refs[pg:2 * pg]
    rout_ref, aout_ref, r_ref, acc_ref = refs[2 * pg:]
    j = pl.program_id(1)
    nq = q_ref.shape[1]
    npg = kp[0].shape[2] * H_B
    sh = int(math.log2(H_B))
    q = q_ref[0].astype(BF16)

    def consts(n):
        j2 = lax.shift_right_logical(lax.broadcasted_iota(I32, (n, n), 0), sh)
        j1 = lax.shift_right_logical(lax.broadcasted_iota(I32, (n, n), 1), sh)
        upper = jnp.where(j2 > j1, 1.0, 0.0).astype(BF16)
        row = lax.broadcasted_iota(I32, (nq, n), 0)
        col = lax.broadcasted_iota(I32, (nq, n), 1)
        same_head = (row & (H_B - 1)) == (col & (H_B - 1))
        before = lax.shift_right_logical(col, sh) < lax.shift_right_logical(row, sh)
        return upper, jnp.ones((n, LANES), BF16), same_head, before

    @pl.when(j == 0)
    def _():
        if with_new:
            r_ref[...] = jnp.zeros_like(r_ref)
            acc_ref[...] = jnp.zeros_like(acc_ref)
            upper, ones, same_head, before = consts(nq)
            _sb_chunk(q, kn_ref[0].astype(BF16), vn_ref[0].astype(BF16), r_ref, acc_ref,
                      upper, ones, jnp.logical_and(same_head, before))
        else:
            r_ref[...] = rin_ref[0]
            acc_ref[...] = ain_ref[0]

    upper, ones, same_head, _ = consts(npg)
    for k in range(pg):
        @pl.when(jnp.max(r_ref[...]) > SB_SKIP)
        def _():
            _sb_chunk(q, kp[k][0, 0].reshape(npg, HEAD_DIM).astype(BF16),
                      vp[k][0, 0].reshape(npg, HEAD_DIM).astype(BF16), r_ref, acc_ref,
                      upper, ones, same_head)

    @pl.when(j == pl.num_programs(1) - 1)
    def _():
        rout_ref[0] = r_ref[...]
        aout_ref[0] = acc_ref[...]


def _sb_sample_call(l, page_table, q, kn, vn, r_in, a_in, pool_k, pool_v, p_hi, n_steps, pg, with_new):
    n_seq, nq, _ = q.shape
    page = pool_k.shape[2]
    row_blk = pl.BlockSpec((1, nq, HEAD_DIM), lambda b, j, pt: (b, 0, 0))
    page_blk = lambda k: pl.BlockSpec((1, 1, page, H_B, HEAD_DIM),
                                      lambda b, j, pt: (l, pt[b, p_hi - (j * pg + k)], 0, 0, 0))
    grid_spec = pltpu.PrefetchScalarGridSpec(
        num_scalar_prefetch=1,
        grid=(n_seq, n_steps),
        in_specs=[row_blk] * 5 + [page_blk(k) for k in range(pg)] * 2,
        out_specs=[row_blk, row_blk],
        scratch_shapes=[pltpu.VMEM((nq, HEAD_DIM), F32), pltpu.VMEM((nq, HEAD_DIM), F32)],
    )
    return pl.pallas_call(
        functools.partial(_sb_sample_kernel, pg=pg, with_new=with_new),
        grid_spec=grid_spec,
        out_shape=[jax.ShapeDtypeStruct((n_seq, nq, HEAD_DIM), F32)] * 2,
        compiler_params=_cparams(("parallel", "arbitrary")),
        name="sb_sample",
    )(page_table, q, kn, vn, r_in, a_in, *([pool_k] * pg), *([pool_v] * pg))


def _sb_sample(l, page_table, q, kn, vn, pool_k, pool_v, pg_first=4):
    n_pages = page_table.shape[1]
    pg_first = min(pg_first, n_pages)
    zeros = jnp.zeros(q.shape, F32)
    r, acc = _sb_sample_call(l, page_table, q, kn, vn, zeros, zeros, pool_k, pool_v,
                             n_pages - 1, 1, pg_first, True)
    rest = n_pages - pg_first
    if rest > 0:
        pg = max(t for t in (4, 2, 1) if rest % t == 0)

        def tail(r, acc):
            return tuple(_sb_sample_call(l, page_table, q, kn, vn, r, acc, pool_k, pool_v,
                                         rest - 1, rest // pg, pg, False))

        r, acc = lax.cond(jnp.max(r) > SB_SKIP, tail, lambda r, acc: (r, acc), r, acc)
    return acc


def _float_key(s):
    bits = lax.bitcast_convert_type(s, I32)
    return bits ^ (lax.shift_right_arithmetic(bits, 31) & 0x7FFFFFFF)


def _topk_threshold(keys_ref, n_chunks, topk, rows):
    width = keys_ref.shape[2]
    sub = min(rows, 16 * SUBLANES)

    def search(r0):
        def bit_step(it, t_u):
            bit = 31 - it
            cand = t_u | lax.shift_left(jnp.int32(1), bit)
            cand_s = jnp.broadcast_to(cand ^ INT_MIN, (sub, LANES))

            def count(c, acc):
                for t in range(width // LANES):
                    ge = keys_ref[c, r0:r0 + sub, t * LANES:(t + 1) * LANES] >= cand_s
                    acc = acc + jnp.where(ge, 1.0, 0.0)
                return acc

            acc = lax.fori_loop(0, n_chunks, count, jnp.zeros((sub, LANES), F32))
            cnt = jnp.sum(acc, axis=1, keepdims=True)
            return jnp.where(cnt >= topk, cand, t_u)

        return lax.fori_loop(0, 32, bit_step, jnp.zeros((sub, 1), I32))

    t_u = jnp.concatenate([search(r0) for r0 in range(0, rows, sub)], axis=0)
    return t_u ^ INT_MIN


def _softmax_update(lg, v_ext, m_ref, acc_ref, idx, n_sub, sub):
    ck = v_ext.shape[0]
    ps, alphas = [], []
    for i in range(n_sub):
        m_old = m_ref[idx(i)]
        m_new = jnp.maximum(m_old, jnp.max(lg[i], axis=1, keepdims=True))
        alphas.append(jnp.exp(m_old - m_new))
        ps.append(jnp.exp(lg[i] - jnp.concatenate([m_new] * (ck // LANES), axis=1)).astype(BF16))
        m_ref[idx(i)] = m_new
    pv = _dot(jnp.concatenate(ps, axis=0), v_ext)
    for i in range(n_sub):
        a2 = jnp.concatenate([alphas[i], alphas[i]], axis=1)
        acc_ref[idx(i)] = acc_ref[idx(i)] * a2 + pv[i * sub:(i + 1) * sub]


def _dsa_prompt_kernel(b31_ref, qi_ref, sm_ref, kk_ref, qa_ref, ka_ref, va_ref, t0_ref, t1_ref,
                       o_ref, keys_ref, qis_ref, qas_ref, m_ref, acc_ref, *, tq, topk):
    ck = tq
    rep = H_A // KV_A
    i = pl.program_id(1)
    w = sm_ref[:, :H_I] * IDX_SCALE
    row = lax.broadcasted_iota(I32, (tq, ck), 0)
    col = lax.broadcasted_iota(I32, (tq, ck), 1)
    causal = col <= row

    for p in range(H_I // 2):
        qis_ref[p * tq:(p + 1) * tq, :] = qi_ref[:, p * LANES:(p + 1) * LANES]
    for h in range(H_A):
        qas_ref[h * tq:(h + 1) * tq, :] = qa_ref[:, h * HEAD_DIM:(h + 1) * HEAD_DIM]

    def scores(c0):
        kd = kk_ref[pl.ds(c0, ck), :]
        kk = jnp.concatenate([kd[:, :LANES], kd[:, LANES:]], axis=0)
        s2 = _dot_nt(qis_ref[...], kk)
        s = jnp.zeros((tq, ck), F32)
        for p in range(H_I // 2):
            blk = s2[p * tq:(p + 1) * tq]
            s = s + jnp.maximum(blk[:, :ck], 0.0) * w[:, 2 * p:2 * p + 1]
            s = s + jnp.maximum(blk[:, ck:], 0.0) * w[:, 2 * p + 1:2 * p + 2]
        return s

    def score_chunk(c, carry):
        keys_ref[c] = _float_key(scores(pl.multiple_of(c * ck, ck)))
        return carry

    lax.fori_loop(0, i, score_chunk, 0)
    i0 = pl.multiple_of(i * ck, ck)
    keys_ref[i] = _float_key(jnp.where(causal, scores(i0), -jnp.inf))

    thr = _topk_threshold(keys_ref, i + 1, topk, tq)

    m_ref[...] = jnp.full(m_ref.shape, M_INIT, F32)
    acc_ref[...] = jnp.zeros_like(acc_ref)
    ones = jnp.ones((ck, HEAD_DIM), BF16)

    def attend(c0, sel, bias_fn):
        kc = ka_ref[pl.ds(c0, ck), :]
        vc = va_ref[pl.ds(c0, ck), :]
        for g in range(KV_A):
            lg_all = _dot_nt(qas_ref[g * rep * tq:(g + 1) * rep * tq, :],
                             kc[:, g * HEAD_DIM:(g + 1) * HEAD_DIM]) * ATT_SCALE
            lg = [jnp.where(sel, lg_all[r * tq:(r + 1) * tq] + bias_fn(g * rep + r), MASKED)
                  for r in range(rep)]
            v_ext = jnp.concatenate([vc[:, g * HEAD_DIM:(g + 1) * HEAD_DIM], ones], axis=1)
            _softmax_update(lg, v_ext, m_ref, acc_ref, lambda r: g * rep + r, rep, tq)

    def far_chunk(c, carry):
        attend(pl.multiple_of(c * ck, ck), keys_ref[c] >= thr, lambda h: b31_ref[h])
        return carry

    lax.fori_loop(0, i - 1, far_chunk, 0)

    @pl.when(i >= 1)
    def _():
        attend(pl.multiple_of((i - 1) * ck, ck), keys_ref[i - 1] >= thr, lambda h: t1_ref[h])

    attend(i0, jnp.logical_and(keys_ref[i] >= thr, causal), lambda h: t0_ref[h])

    for h in range(H_A):
        a = acc_ref[h]
        o_ref[:, h * HEAD_DIM:(h + 1) * HEAD_DIM] = (a[:, :HEAD_DIM] / a[:, HEAD_DIM:]).astype(o_ref.dtype)


def _bucket_table(n):
    d = np.arange(n)
    max_exact = NUM_BUCKETS // 2
    nf = np.maximum(d, 1).astype(np.float32)
    large = max_exact + (np.log(nf / np.float32(max_exact)) / np.float32(math.log(MAX_DISTANCE / max_exact))
                         * np.float32(NUM_BUCKETS - max_exact)).astype(np.int32)
    large = np.minimum(large, NUM_BUCKETS - 1)
    return np.where(d < max_exact, d, large).astype(np.int32)


def _bias_by_distance(rel_bias, dist):
    buckets = jnp.asarray(_bucket_table(int(dist.max()) + 1)[dist])
    out = jnp.zeros((rel_bias.shape[1],) + dist.shape, F32)
    for b in range(NUM_BUCKETS):
        out = jnp.where(buckets[None] == b, rel_bias[b].reshape((-1,) + (1,) * dist.ndim), out)
    return out


def _bias_tiles(rel_bias, tq):
    r = np.arange(tq)[:, None]
    j = np.arange(tq)[None, :]
    t0 = _bias_by_distance(rel_bias, np.maximum(r - j, 0))
    t1 = _bias_by_distance(rel_bias, tq + r - j)
    return t0, t1, rel_bias[NUM_BUCKETS - 1]


def _dsa_prompt(zf, zb, rel_bias, n_batch, seq, off_ka, off_va, off_sm, tq=256):
    topk = min(TOPK_MAX, seq // 4)
    nq = seq // tq
    t0, t1, b31 = _bias_tiles(rel_bias, tq)
    assert _bucket_table(4 * tq)[tq + 1:].min() == NUM_BUCKETS - 1
    return pl.pallas_call(
        functools.partial(_dsa_prompt_kernel, tq=tq, topk=topk),
        grid=(n_batch, nq),
        in_specs=[
            pl.BlockSpec(memory_space=pltpu.SMEM),
            pl.BlockSpec((tq, _QI), lambda b, i: (b * nq + i, OFF_QI // _QI)),
            pl.BlockSpec((tq, LANES), lambda b, i: (b * nq + i, off_sm // LANES + 2)),
            pl.BlockSpec((seq, 2 * LANES), lambda b, i: (b, off_sm // (2 * LANES))),
            pl.BlockSpec((tq, _QA), lambda b, i: (b * nq + i, OFF_QA // _QA)),
            pl.BlockSpec((seq, _KVA), lambda b, i: (b, off_ka // _KVA)),
            pl.BlockSpec((seq, _KVA), lambda b, i: (b, off_va // _KVA)),
            pl.BlockSpec((H_A, tq, tq), lambda b, i: (0, 0, 0)),
            pl.BlockSpec((H_A, tq, tq), lambda b, i: (0, 0, 0)),
        ],
        out_specs=pl.BlockSpec((tq, _QA), lambda b, i: (b * nq + i, 0)),
        out_shape=jax.ShapeDtypeStruct((n_batch * seq, _QA), BF16),
        scratch_shapes=[pltpu.VMEM((nq, tq, tq), I32),
                        pltpu.VMEM((H_I // 2 * tq, LANES), BF16),
                        pltpu.VMEM((H_A * tq, HEAD_DIM), BF16),
                        pltpu.VMEM((H_A, tq, LANES), F32),
                        pltpu.VMEM((H_A, tq, 2 * HEAD_DIM), F32)],
        compiler_params=_cparams(("parallel", "arbitrary")),
        name="dsa_prompt",
    )(b31, zb, zf, zb, zb, zb, zb, t0, t1)


def _idx_sample_kernel(pt_ref, q_ref, w_ref, kin_ref, *refs, pg):
    pages = refs[:pg]
    s_ref, snew_ref = refs[pg:]
    dec = s_ref.shape[2]
    q = q_ref[0]
    w = w_ref[0] * IDX_SCALE

    def score(k):
        s2 = _dot_nt(q, k.astype(BF16))
        s = jnp.zeros((dec, s2.shape[1]), F32)
        for h in range(H_I):
            s = s + jnp.maximum(s2[h * dec:(h + 1) * dec], 0.0) * w[h * dec:(h + 1) * dec]
        return s

    for k in range(pg):
        s_ref[0, k] = score(pages[k][0, 0])

    @pl.when(pl.program_id(1) == 0)
    def _():
        row = lax.broadcasted_iota(I32, snew_ref.shape[1:], 0)
        col = lax.broadcasted_iota(I32, snew_ref.shape[1:], 1)
        snew_ref[0] = jnp.where(col <= row, score(kin_ref[0]), -jnp.inf)


def _idx_sample(l, page_table, q, w, kin, pool_ki, pg):
    n_seq, n_pages = page_table.shape
    dec = q.shape[1] // H_I
    page = pool_ki.shape[2]
    full = lambda shp: pl.BlockSpec((1,) + shp, lambda b, j, pt: (b, 0, 0))
    page_blk = lambda k: pl.BlockSpec((1, 1, page, D_I), lambda b, j, pt: (l, pt[b, j * pg + k], 0, 0))
    grid_spec = pltpu.PrefetchScalarGridSpec(
        num_scalar_prefetch=1,
        grid=(n_seq, n_pages // pg),
        in_specs=[full(q.shape[1:]), full(w.shape[1:]), full(kin.shape[1:])] + [page_blk(k) for k in range(pg)],
        out_specs=[pl.BlockSpec((1, pg, dec, page), lambda b, j, pt: (b, j, 0, 0)),
                   pl.BlockSpec((1, dec, page), lambda b, j, pt: (b, 0, 0))],
    )
    return pl.pallas_call(
        functools.partial(_idx_sample_kernel, pg=pg),
        grid_spec=grid_spec,
        out_shape=[jax.ShapeDtypeStruct((n_seq, n_pages, dec, page), F32),
                   jax.ShapeDtypeStruct((n_seq, dec, page), F32)],
        compiler_params=_cparams(("parallel", "arbitrary")),
        name="idx_sample",
    )(page_table, q, w, kin, *([pool_ki] * pg))


def _dsa_sample_kernel(pt_ref, s_ref, snew_ref, qa_ref, kn_ref, vn_ref, bcol_ref, bpage_ref, bnew_ref,
                       *refs, pg, topk):
    kp, vp = refs[:pg], refs[pg:2 * pg]
    o_ref, keys_ref, thr_ref, m_ref, acc_ref = refs[2 * pg:]
    j = pl.program_id(1)
    last = pl.num_programs(1) - 1
    n_pages = s_ref.shape[1]
    dec, page = s_ref.shape[2], s_ref.shape[3]
    ones = jnp.ones((page, HEAD_DIM), BF16)

    @pl.when(j == 0)
    def _():
        def to_keys(p, carry):
            keys_ref[p] = _float_key(s_ref[0, p])
            return carry

        lax.fori_loop(0, n_pages, to_keys, 0)
        keys_ref[n_pages] = _float_key(snew_ref[0])
        thr = _topk_threshold(keys_ref, n_pages + 1, topk, dec)
        thr_ref[...] = jnp.broadcast_to(thr, thr_ref.shape)
        m_ref[...] = jnp.full(m_ref.shape, M_INIT, F32)
        acc_ref[...] = jnp.zeros_like(acc_ref)

    def tile_rows_cols(sb):
        sb = jnp.concatenate([sb] * (H_A), axis=0)
        return jnp.concatenate([sb] * KV_A, axis=1)

    def fold(k_groups, v_groups, bias):
        k_cat = jnp.concatenate([t for pg_tiles in k_groups for t in pg_tiles], axis=0)
        v_cat = jnp.concatenate([jnp.concatenate([t, ones], axis=1) for pg_tiles in v_groups for t in pg_tiles],
                                axis=0)
        lg = _dot_nt(qa_ref[0], k_cat) * ATT_SCALE + bias
        _softmax_update([lg], v_cat, m_ref, acc_ref, lambda _: slice(None), 1, qa_ref.shape[1])

    biases = []
    for k in range(pg):
        sb = tile_rows_cols(jnp.where(keys_ref[j * pg + k] >= thr_ref[...], 0.0, MASKED))
        if k == pg - 1:
            bias = jnp.where(j == last, bpage_ref[...], bcol_ref[...])
        else:
            bias = bcol_ref[...]
        biases.append(bias + sb)
    group = lambda ref, g: ref[0, 0, :, g * HEAD_DIM:(g + 1) * HEAD_DIM].astype(BF16)
    fold([[group(kp[k], g) for g in range(KV_A)] for k in range(pg)],
         [[group(vp[k], g) for g in range(KV_A)] for k in range(pg)],
         jnp.concatenate(biases, axis=1))

    @pl.when(j == last)
    def _():
        row = lax.broadcasted_iota(I32, (dec, page), 0)
        col = lax.broadcasted_iota(I32, (dec, page), 1)
        ok = jnp.logical_and(keys_ref[n_pages] >= thr_ref[...], col <= row)
        sb = tile_rows_cols(jnp.where(ok, 0.0, MASKED))
        kn = kn_ref[0].astype(BF16)
        vn = vn_ref[0].astype(BF16)
        fold([[kn[:, g * HEAD_DIM:(g + 1) * HEAD_DIM] for g in range(KV_A)]],
             [[vn[:, g * HEAD_DIM:(g + 1) * HEAD_DIM] for g in range(KV_A)]], bnew_ref[...] + sb)
        a = acc_ref[...]
        o_ref[0] = a[:, :HEAD_DIM] / a[:, HEAD_DIM:]


def _dsa_sample(l, page_table, s, snew, qa, kn, vn, biases, pool_k, pool_v, pg):
    n_seq, n_pages = page_table.shape
    dec, page = s.shape[2], s.shape[3]
    rows = qa.shape[1]
    topk = min(TOPK_MAX, (n_pages * page + dec) // 4)
    full2 = lambda a: pl.BlockSpec(a.shape, lambda b, j, pt: (0, 0))
    page_blk = lambda k: pl.BlockSpec((1, 1, page, _KVA), lambda b, j, pt: (l, pt[b, j * pg + k], 0, 0))
    grid_spec = pltpu.PrefetchScalarGridSpec(
        num_scalar_prefetch=1,
        grid=(n_seq, n_pages // pg),
        in_specs=[pl.BlockSpec((1, n_pages, dec, page), lambda b, j, pt: (b, 0, 0, 0)),
                  pl.BlockSpec((1, dec, page), lambda b, j, pt: (b, 0, 0)),
                  pl.BlockSpec((1, rows, HEAD_DIM), lambda b, j, pt: (b, 0, 0)),
                  pl.BlockSpec((1, page, _KVA), lambda b, j, pt: (b, 0, 0)),
                  pl.BlockSpec((1, page, _KVA), lambda b, j, pt: (b, 0, 0))]
                 + [full2(a) for a in biases] + [page_blk(k) for k in range(pg)] * 2,
        out_specs=pl.BlockSpec((1, rows, HEAD_DIM), lambda b, j, pt: (b, 0, 0)),
        scratch_shapes=[pltpu.VMEM((n_pages + 1, dec, page), I32),
                        pltpu.VMEM((dec, page), I32),
                        pltpu.VMEM((rows, LANES), F32),
                        pltpu.VMEM((rows, 2 * HEAD_DIM), F32)],
    )
    return pl.pallas_call(
        functools.partial(_dsa_sample_kernel, pg=pg, topk=topk),
        grid_spec=grid_spec,
        out_shape=jax.ShapeDtypeStruct((n_seq, rows, HEAD_DIM), F32),
        compiler_params=_cparams(("parallel", "arbitrary")),
        name="dsa_sample",
    )(page_table, s, snew, qa, kn, vn, *biases, *([pool_k] * pg), *([pool_v] * pg))


def _sample_bias(rel_bias, dec, page):
    assert _bucket_table(4 * page)[page + 1:].min() == NUM_BUCKETS - 1
    t = np.arange(dec)[:, None]
    jc = np.arange(page)[None, :]
    rep = H_A // KV_A
    row_g = np.arange(H_A * dec)[:, None] // (rep * dec)
    col_g = np.arange(KV_A * page)[None, :] // page
    other = jnp.asarray(np.where(row_g == col_g, 0.0, MASKED), F32)

    def expand(b):
        b = b.reshape(H_A * dec, page)
        return jnp.concatenate([b] * KV_A, axis=1) + other

    bpage = expand(_bias_by_distance(rel_bias, page + t - jc))
    bnew = expand(_bias_by_distance(rel_bias, np.maximum(t - jc, 0)))
    bcol = expand(jnp.broadcast_to(rel_bias[NUM_BUCKETS - 1][:, None, None], (H_A, dec, page)))
    return bcol, bpage, bnew


def _merge_kernel(oa_ref, ob_ref, wa_ref, wb_ref, ga_ref, gb_ref, m_ref):
    pa = _dot(oa_ref[...], wa_ref[...].astype(BF16))
    pb = _dot(ob_ref[...], wb_ref[...].astype(BF16))
    m_ref[...] = (jax.nn.sigmoid(ga_ref[...]) * pa + jax.nn.sigmoid(gb_ref[...]) * pb).astype(m_ref.dtype)


def _merge(oa, ob, w_pa, w_pb, zf, off_ga, off_gb, tm, tn=512):
    m, ka = oa.shape
    kb = ob.shape[1]
    n = w_pa.shape[1]
    return pl.pallas_call(
        _merge_kernel,
        grid=(m // tm, n // tn),
        in_specs=[pl.BlockSpec((tm, ka), lambda i, j: (i, 0)),
                  pl.BlockSpec((tm, kb), lambda i, j: (i, 0)),
                  pl.BlockSpec((ka, tn), lambda i, j: (0, j)),
                  pl.BlockSpec((kb, tn), lambda i, j: (0, j)),
                  pl.BlockSpec((tm, tn), lambda i, j: (i, off_ga // tn + j)),
                  pl.BlockSpec((tm, tn), lambda i, j: (i, off_gb // tn + j))],
        out_specs=pl.BlockSpec((tm, tn), lambda i, j: (i, j)),
        out_shape=jax.ShapeDtypeStruct((m, n), BF16),
        compiler_params=_cparams(("parallel", "arbitrary")),
        name="merge",
    )(oa, ob, w_pa, w_pb, zf, zf)


def _mm_ln_kernel(x_ref, a_ref, w_ref, g_ref, b_ref, yf_ref, yb_ref, acc_ref, *, alpha):
    k = pl.program_id(1)

    @pl.when(k == 0)
    def _():
        acc_ref[...] = jnp.zeros_like(acc_ref)

    acc_ref[...] += _dot(a_ref[...].astype(BF16), w_ref[...])

    @pl.when(k == pl.num_programs(1) - 1)
    def _():
        y = alpha * x_ref[...] + acc_ref[...]
        mu = jnp.mean(y, axis=-1, keepdims=True)
        yc = y - mu
        var = jnp.mean(yc * yc, axis=-1, keepdims=True)
        out = yc * lax.rsqrt(var + LN_EPS) * g_ref[...] + b_ref[...]
        yf_ref[...] = out
        yb_ref[...] = out.astype(BF16)


def _mm_ln(x, a, w, g, b, alpha, tm):
    m, n = x.shape
    kdim = a.shape[1]
    tk = _largest_tile(kdim, 1408)
    return pl.pallas_call(
        functools.partial(_mm_ln_kernel, alpha=alpha),
        grid=(m // tm, kdim // tk),
        in_specs=[pl.BlockSpec((tm, n), lambda i, k: (i, 0)),
                  pl.BlockSpec((tm, tk), lambda i, k: (i, k)),
                  pl.BlockSpec((tk, n), lambda i, k: (k, 0)),
                  pl.BlockSpec((1, n), lambda i, k: (0, 0)),
                  pl.BlockSpec((1, n), lambda i, k: (0, 0))],
        out_specs=[pl.BlockSpec((tm, n), lambda i, k: (i, 0)),
                   pl.BlockSpec((tm, n), lambda i, k: (i, 0))],
        out_shape=[jax.ShapeDtypeStruct((m, n), F32), jax.ShapeDtypeStruct((m, n), BF16)],
        scratch_shapes=[pltpu.VMEM((tm, n), F32)],
        compiler_params=_cparams(("parallel", "arbitrary")),
        name="mm_ln",
    )(x, a, w, g.reshape(1, n), b.reshape(1, n))


def _gelu_tanh(x):
    return 0.5 * x * (1.0 + jnp.tanh(math.sqrt(2.0 / math.pi) * (x + 0.044715 * (x * x * x))))


def _ffn_up_kernel(x_ref, wa_ref, wv_ref, cw_ref, cb_ref, prev_ref, h_ref, tail_ref, ext_ref,
                   *, tm, tiles_per_seq):
    i = pl.program_id(1)
    wa = wa_ref[...].astype(BF16)
    wv = wv_ref[...].astype(BF16)
    cw = cw_ref[...]
    cb = cb_ref[...]

    @pl.when(i % tiles_per_seq == 0)
    def _():
        ext_ref[0:SUBLANES, :] = prev_ref[0]

    rc = min(tm, 256)
    for r0 in range(0, tm, rc):
        x = x_ref[r0:r0 + rc, :].astype(BF16)
        a = _dot(x, wa)
        val = _dot(x, wv)
        e0 = SUBLANES + r0
        ext_ref[e0:e0 + rc, :] = a
        c = (cb
             + cw[0:1, :] * ext_ref[e0 - 2:e0 - 2 + rc, :]
             + cw[1:2, :] * ext_ref[e0 - 1:e0 - 1 + rc, :]
             + cw[2:3, :] * a)
        h_ref[r0:r0 + rc, :] = (_gelu_tanh(c) * val).astype(h_ref.dtype)
    last = ext_ref[tm:tm + SUBLANES, :]
    tail_ref[0] = last
    ext_ref[0:SUBLANES, :] = last


def _ffn_up(x, w_up, conv_w, conv_b, prev8, tm, rows_per_seq, h_dtype, tf=512):
    m, d = x.shape
    f = conv_w.shape[1]
    nf = f // tf
    tiles_per_seq = rows_per_seq // tm
    return pl.pallas_call(
        functools.partial(_ffn_up_kernel, tm=tm, tiles_per_seq=tiles_per_seq),
        grid=(nf, m // tm),
        in_specs=[pl.BlockSpec((tm, d), lambda j, i: (i, 0)),
                  pl.BlockSpec((d, tf), lambda j, i: (0, j)),
                  pl.BlockSpec((d, tf), lambda j, i: (0, nf + j)),
                  pl.BlockSpec((CONV_W, tf), lambda j, i: (0, j)),
                  pl.BlockSpec((1, tf), lambda j, i: (0, j)),
                  pl.BlockSpec((1, SUBLANES, tf), lambda j, i: (i // tiles_per_seq, 0, j))],
        out_specs=[pl.BlockSpec((tm, tf), lambda j, i: (i, j)),
                   pl.BlockSpec((1, SUBLANES, tf), lambda j, i: (i // tiles_per_seq, 0, j))],
        out_shape=[jax.ShapeDtypeStruct((m, f), h_dtype),
                   jax.ShapeDtypeStruct((m // rows_per_seq, SUBLANES, f), F32)],
        scratch_shapes=[pltpu.VMEM((tm + SUBLANES, tf), F32)],
        compiler_params=_cparams(("parallel", "arbitrary")),
        name="ffn_up",
    )(x, w_up, w_up, conv_w, conv_b.reshape(1, f), prev8)


def _pad_w_in(w_in):
    sizes = (_QA, _KVA, _KVA, _QI, D_I, H_I, _QB, _QB, _QB)
    offs = np.cumsum((0,) + sizes)
    qa, ka, va, qi, ki, wi, qb, kb, vb = [w_in[..., offs[t]:offs[t + 1]] for t in range(9)]
    g = w_in[..., offs[9]:]
    zk = jnp.zeros_like(ki)
    pad = jnp.zeros(w_in.shape[:-1] + (SMALL_W - 4 * D_I - H_I,), w_in.dtype)
    return jnp.concatenate([qa, qi, qb, kb, vb, g, ka, va, ki, zk, zk, ki, wi, pad], axis=-1).astype(BF16)


def _pad_rows(a, rows):
    return jnp.pad(a, ((0, 0), (0, rows - a.shape[1]), (0, 0)))


def kernel(x_prompt, x_sample, cache_k_a, cache_v_a, cache_k_idx, cache_k_b, cache_v_b, state_conv,
           page_table, rel_bias, w_in, w_proj_a, w_proj_b, w_out, ln1_g, ln1_b, w_up, conv_w, conv_b,
           w_down, ln2_g, ln2_b):
    depth = w_in.shape[0]
    n_p, seq, d_model = x_prompt.shape
    n_s, dec = x_sample.shape[:2]
    d_ff = conv_w.shape[-1]
    n_pool, page = cache_k_a.shape[1:3]
    n_pages = page_table.shape[1]
    rep = H_A // KV_A
    alpha = (2 * depth) ** 0.25
    assert dec == SUBLANES and page == LANES

    off_ga = OFF_VB + _QB
    off_gb = off_ga + d_model
    off_ka = off_gb + d_model
    off_va = off_ka + _KVA
    off_sm = off_va + _KVA
    off_wi = off_sm + 2 * LANES
    w_in_p = _pad_w_in(w_in)
    p_pad = w_in_p.shape[-1]
    tn_in = _largest_tile(p_pad, 1024)
    w_out_b = w_out.astype(BF16)
    w_down_b = w_down.astype(BF16)

    pool_ka = cache_k_a.reshape(depth, n_pool, page, _KVA)
    pool_va = cache_v_a.reshape(depth, n_pool, page, _KVA)
    pool_kb, pool_vb = cache_k_b, cache_v_b
    pg_a = max(t for t in (16, 8, 4, 2, 1) if n_pages % t == 0)
    sample_bias = _sample_bias(rel_bias, dec, page)

    mp = n_p * seq
    ms = n_s * dec
    xp_f = x_prompt.reshape(mp, d_model)
    xp_b = xp_f.astype(BF16)
    xs_f = x_sample.reshape(ms, d_model)
    tm_p = min(1024, seq)
    tm_ln = min(512, seq)
    zero_prev = jnp.zeros((n_p, SUBLANES, d_ff), F32)

    outs = [[] for _ in range(12)]
    for l in range(depth):
        zf, zb = _inproj(xp_b, w_in_p[l], tm_p, tn_in)
        oa = _dsa_prompt(zf, zb, rel_bias, n_p, seq, off_ka, off_va, off_sm)
        ob = _sb_prompt(zb, n_p, seq)
        mg = _merge(oa, ob, w_proj_a[l], w_proj_b[l], zf, off_ga, off_gb, tm_p)
        xp_f, xp_b = _mm_ln(xp_f, mg, w_out_b[l], ln1_g[l], ln1_b[l], alpha, tm_ln)
        h, tail = _ffn_up(xp_b, w_up[l], conv_w[l], conv_b[l], zero_prev, tm_p, seq, BF16)
        xp_f, xp_b = _mm_ln(xp_f, h, w_down_b[l], ln2_g[l], ln2_b[l], alpha, tm_ln)
        outs[0].append(zf[:, off_ka:off_ka + _KVA].reshape(n_p, seq, KV_A, HEAD_DIM))
        outs[1].append(zf[:, off_va:off_va + _KVA].reshape(n_p, seq, KV_A, HEAD_DIM))
        outs[2].append(zf[:, off_sm:off_sm + D_I].reshape(n_p, seq, D_I))
        outs[3].append(zf[:, OFF_KB:OFF_KB + _QB].reshape(n_p, seq, H_B, HEAD_DIM))
        outs[4].append(zf[:, OFF_VB:OFF_VB + _QB].reshape(n_p, seq, H_B, HEAD_DIM))
        outs[5].append(tail[:, SUBLANES - (CONV_W - 1):, :])

        zs, _ = _inproj(xs_f, w_in_p[l], ms, tn_in)
        z3 = zs.reshape(n_s, dec, p_pad)
        ka = z3[..., off_ka:off_ka + _KVA]
        va = z3[..., off_va:off_va + _KVA]
        ki = z3[..., off_sm:off_sm + D_I]
        kb = z3[..., OFF_KB:OFF_KB + _QB]
        vb = z3[..., OFF_VB:OFF_VB + _QB]
        qi_s = (z3[..., OFF_QI:OFF_QI + _QI].reshape(n_s, dec, H_I, D_I).transpose(0, 2, 1, 3)
                .reshape(n_s, H_I * dec, D_I).astype(BF16))
        wi_s = z3[..., off_wi:off_wi + H_I].transpose(0, 2, 1).reshape(n_s, H_I * dec, 1)
        qa_s = (z3[..., OFF_QA:OFF_QA + _QA].reshape(n_s, dec, KV_A, rep, HEAD_DIM).transpose(0, 2, 3, 1, 4)
                .reshape(n_s, H_A * dec, HEAD_DIM).astype(BF16))
        s_past, s_new = _idx_sample(l, page_table, qi_s, wi_s, _pad_rows(ki, page), cache_k_idx, pg_a)
        oa_s = _dsa_sample(l, page_table, s_past, s_new, qa_s, _pad_rows(ka, page), _pad_rows(va, page),
                           sample_bias, pool_ka, pool_va, pg_a)
        oa_s = (oa_s.reshape(n_s, KV_A, rep, dec, HEAD_DIM).transpose(0, 3, 1, 2, 4)
                .reshape(ms, _QA).astype(BF16))
        ob_s = _sb_sample(l, page_table, z3[..., OFF_QB:OFF_QB + _QB].reshape(n_s, dec * H_B, HEAD_DIM),
                          kb.reshape(n_s, dec * H_B, HEAD_DIM), vb.reshape(n_s, dec * H_B, HEAD_DIM),
                          pool_kb, pool_vb)
        ob_s = ob_s.reshape(ms, _QB).astype(BF16)
        mg_s = _merge(oa_s, ob_s, w_proj_a[l], w_proj_b[l], zs, off_ga, off_gb, ms)
        xs_f, _ = _mm_ln(xs_f, mg_s, w_out_b[l], ln1_g[l], ln1_b[l], alpha, ms)
        prev8 = jnp.concatenate([jnp.zeros((n_s, SUBLANES - (CONV_W - 1), d_ff), F32), state_conv[l]], axis=1)
        h_s, tail_s = _ffn_up(xs_f, w_up[l], conv_w[l], conv_b[l], prev8, dec, dec, F32)
        xs_f, _ = _mm_ln(xs_f, h_s, w_down_b[l], ln2_g[l], ln2_b[l], alpha, ms)
        outs[6].append(ka.reshape(n_s, dec, KV_A, HEAD_DIM))
        outs[7].append(va.reshape(n_s, dec, KV_A, HEAD_DIM))
        outs[8].append(ki)
        outs[9].append(kb.reshape(n_s, dec, H_B, HEAD_DIM))
        outs[10].append(vb.reshape(n_s, dec, H_B, HEAD_DIM))
        outs[11].append(tail_s[:, SUBLANES - (CONV_W - 1):, :])

    return (xp_f.reshape(n_p, seq, d_model), xs_f.reshape(n_s, dec, d_model)) + tuple(jnp.stack(o) for o in outs)
```

```python
import functools
import math

import numpy as np
import jax
import jax.numpy as jnp
from jax import lax
from jax.experimental import pallas as pl
from jax.experimental.pallas import tpu as pltpu

F32 = jnp.float32
BF16 = jnp.bfloat16
I32 = jnp.int32

HEAD_DIM = 128
H_A = 8
KV_A = 2
H_I = 16
D_I = 64
TOPK_MAX = 256
H_B = 8
NUM_BUCKETS = 32
MAX_DISTANCE = 128
CONV_W = 3
LN_EPS = 1e-5

LANES = 128
SUBLANES = 8
VMEM_LIMIT = 56 * 1024 * 1024

_QA = H_A * HEAD_DIM
_QI = H_I * D_I
_QB = H_B * HEAD_DIM
_KVA = KV_A * HEAD_DIM
OFF_QA = 0
OFF_QI = OFF_QA + _QA
OFF_QB = OFF_QI + _QI
OFF_KB = OFF_QB + _QB
OFF_VB = OFF_KB + _QB
SMALL_W = 4 * LANES
INT_MIN = -2 ** 31
MASKED = -1e30
M_INIT = -1e20
SB_SKIP = -105.0
ATT_SCALE = HEAD_DIM ** -0.5
IDX_SCALE = (D_I ** -0.5) * (H_I ** -0.5)


def _cparams(sem):
    return pltpu.CompilerParams(dimension_semantics=sem, vmem_limit_bytes=VMEM_LIMIT)


def _dot(a, b):
    return jnp.dot(a, b, preferred_element_type=F32)


def _dot_nt(a, b):
    return lax.dot_general(a, b, (((1,), (1,)), ((), ())), preferred_element_type=F32)


def _largest_tile(n, cap):
    return max(t for t in range(LANES, cap + 1, LANES) if n % t == 0)


def _inproj_kernel(x_ref, w_ref, zf_ref, zb_ref):
    z = _dot(x_ref[...].astype(BF16), w_ref[...])
    zf_ref[...] = z
    zb_ref[...] = z.astype(BF16)


def _inproj(x, w, tm, tn):
    m, k = x.shape
    n = w.shape[1]
    return pl.pallas_call(
        _inproj_kernel,
        grid=(m // tm, n // tn),
        in_specs=[pl.BlockSpec((tm, k), lambda i, j: (i, 0)),
                  pl.BlockSpec((k, tn), lambda i, j: (0, j))],
        out_specs=[pl.BlockSpec((tm, tn), lambda i, j: (i, j)),
                   pl.BlockSpec((tm, tn), lambda i, j: (i, j))],
        out_shape=[jax.ShapeDtypeStruct((m, n), F32), jax.ShapeDtypeStruct((m, n), BF16)],
        compiler_params=_cparams(("parallel", "arbitrary")),
        name="inproj",
    )(x, w)


def _sb_chunk(q, k, v, r_ref, acc_ref, upper, ones, mask):
    n = k.shape[0]
    z = _dot_nt(q, k) * ATT_SCALE
    lk = -(jnp.maximum(z, 0.0) + jnp.log1p(jnp.exp(-jnp.abs(z))))
    if mask is not None:
        lk = jnp.where(mask, lk, 0.0)
    hi = lk.astype(BF16)
    lo = (lk - hi.astype(F32)).astype(BF16)
    r = r_ref[...]
    w = r.shape[1]
    rb = r[:, :n] if n <= w else jnp.concatenate([r] * (n // w), axis=1)
    after = _dot(hi, upper) + _dot(lo, upper) + rb
    tot = _dot(hi, ones) + _dot(lo, ones)
    a = jnp.exp(z + lk + after)
    if mask is not None:
        a = jnp.where(mask, a, 0.0)
    acc_ref[...] += _dot(a.astype(BF16), v)
    r_new = r + tot
    r_ref[...] = r_new
    return jnp.max(r_new)


def _sb_prompt_kernel(q_ref, k_ref, v_ref, o_ref, r_ref, acc_ref, *, tq, nh):
    seq = q_ref.shape[0]
    row = lax.broadcasted_iota(I32, (tq, tq), 0)
    col = lax.broadcasted_iota(I32, (tq, tq), 1)
    upper = jnp.where(row > col, 1.0, 0.0).astype(BF16)
    ones = jnp.ones((tq, tq), BF16)
    strict = col < row
    head = lambda ref, r0, h: ref[pl.ds(r0, tq), h * HEAD_DIM:(h + 1) * HEAD_DIM]

    def qblock(i, carry):
        r0 = pl.multiple_of(i * tq, tq)
        qs = [head(q_ref, r0, h) for h in range(nh)]
        r_ref[...] = jnp.zeros_like(r_ref)
        acc_ref[...] = jnp.zeros_like(acc_ref)

        def chunk(c0, mask):
            rms = [_sb_chunk(qs[h], head(k_ref, c0, h), head(v_ref, c0, h),
                             r_ref.at[h], acc_ref.at[h], upper, ones, mask) for h in range(nh)]
            return functools.reduce(jnp.maximum, rms)

        rmax = chunk(r0, strict)

        def cond(st):
            c, rm = st
            return jnp.logical_and(c >= 0, rm > SB_SKIP)

        def body(st):
            c, _ = st
            return c - 1, chunk(pl.multiple_of(c * tq, tq), None)

        lax.while_loop(cond, body, (i - 1, rmax))
        for h in range(nh):
            o_ref[pl.ds(r0, tq), h * HEAD_DIM:(h + 1) * HEAD_DIM] = acc_ref[h].astype(o_ref.dtype)
        return carry

    lax.fori_loop(0, seq // tq, qblock, 0)


def _sb_prompt(zb, n_batch, seq, tq=256, nh=2):
    width = nh * HEAD_DIM
    blk = lambda off: pl.BlockSpec((seq, width), lambda b, h: (b, off // width + h))
    return pl.pallas_call(
        functools.partial(_sb_prompt_kernel, tq=tq, nh=nh),
        grid=(n_batch, H_B // nh),
        in_specs=[blk(OFF_QB), blk(OFF_KB), blk(OFF_VB)],
        out_specs=pl.BlockSpec((seq, width), lambda b, h: (b, h)),
        out_shape=jax.ShapeDtypeStruct((n_batch * seq, H_B * HEAD_DIM), BF16),
        scratch_shapes=[pltpu.VMEM((nh, tq, tq), F32), pltpu.VMEM((nh, tq, HEAD_DIM), F32)],
        compiler_params=_cparams(("parallel", "parallel")),
        name="sb_prompt",
    )(zb, zb, zb)


def _sb_sample_kernel(pt_ref, q_ref, kn_ref, vn_ref, rin_ref, ain_ref, *refs, pg, with_new):
    kp, vp = refs[:pg], refs[pg:2 * pg]
    rout_ref, aout_ref, r_ref, acc_ref = refs[2 * pg:]
    j = pl.program_id(1)
    nq = q_ref.shape[1]
    npg = kp[0].shape[2] * H_B
    sh = int(math.log2(H_B))
    q = q_ref[0].astype(BF16)

    def consts(n):
        j2 = lax.shift_right_logical(lax.broadcasted_iota(I32, (n, n), 0), sh)
        j1 = lax.shift_right_logical(lax.broadcasted_iota(I32, (n, n), 1), sh)
        upper = jnp.where(j2 > j1, 1.0, 0.0).astype(BF16)
        row = lax.broadcasted_iota(I32, (nq, n), 0)
        col = lax.broadcasted_iota(I32, (nq, n), 1)
        same_head = (row & (H_B - 1)) == (col & (H_B - 1))
        before = lax.shift_right_logical(col, sh) < lax.shift_right_logical(row, sh)
        return upper, jnp.ones((n, LANES), BF16), same_head, before

    @pl.when(j == 0)
    def _():
        if with_new:
            r_ref[...] = jnp.zeros_like(r_ref)
            acc_ref[...] = jnp.zeros_like(acc_ref)
            upper, ones, same_head, before = consts(nq)
            _sb_chunk(q, kn_ref[0].astype(BF16), vn_ref[0].astype(BF16), r_ref, acc_ref,
                      upper, ones, jnp.logical_and(same_head, before))
        else:
            r_ref[...] = rin_ref[0]
            acc_ref[...] = ain_ref[0]

    upper, ones, same_head, _ = consts(npg)
    for k in range(pg):
        @pl.when(jnp.max(r_ref[...]) > SB_SKIP)
        def _():
            _sb_chunk(q, kp[k][0, 0].reshape(npg, HEAD_DIM).astype(BF16),
                      vp[k][0, 0].reshape(npg, HEAD_DIM).astype(BF16), r_ref, acc_ref,
                      upper, ones, same_head)

    @pl.when(j == pl.num_programs(1) - 1)
    def _():
        rout_ref[0] = r_ref[...]
        aout_ref[0] = acc_ref[...]


def _sb_sample_call(l, page_table, q, kn, vn, r_in, a_in, pool_k, pool_v, p_hi, n_steps, pg, with_new):
    n_seq, nq, _ = q.shape
    page = pool_k.shape[2]
    row_blk = pl.BlockSpec((1, nq, HEAD_DIM), lambda b, j, pt: (b, 0, 0))
    page_blk = lambda k: pl.BlockSpec((1, 1, page, H_B, HEAD_DIM),
                                      lambda b, j, pt: (l, pt[b, p_hi - (j * pg + k)], 0, 0, 0))
    grid_spec = pltpu.PrefetchScalarGridSpec(
        num_scalar_prefetch=1,
        grid=(n_seq, n_steps),
        in_specs=[row_blk] * 5 + [page_blk(k) for k in range(pg)] * 2,
        out_specs=[row_blk, row_blk],
        scratch_shapes=[pltpu.VMEM((nq, HEAD_DIM), F32), pltpu.VMEM((nq, HEAD_DIM), F32)],
    )
    return pl.pallas_call(
        functools.partial(_sb_sample_kernel, pg=pg, with_new=with_new),
        grid_spec=grid_spec,
        out_shape=[jax.ShapeDtypeStruct((n_seq, nq, HEAD_DIM), F32)] * 2,
        compiler_params=_cparams(("parallel", "arbitrary")),
        name="sb_sample",
    )(page_table, q, kn, vn, r_in, a_in, *([pool_k] * pg), *([pool_v] * pg))


def _sb_sample(l, page_table, q, kn, vn, pool_k, pool_v, pg_first=4):
    n_pages = page_table.shape[1]
    pg_first = min(pg_first, n_pages)
    zeros = jnp.zeros(q.shape, F32)
    r, acc = _sb_sample_call(l, page_table, q, kn, vn, zeros, zeros, pool_k, pool_v,
                             n_pages - 1, 1, pg_first, True)
    rest = n_pages - pg_first
    if rest > 0:
        pg = max(t for t in (4, 2, 1) if rest % t == 0)

        def tail(r, acc):
            return tuple(_sb_sample_call(l, page_table, q, kn, vn, r, acc, pool_k, pool_v,
                                         rest - 1, rest // pg, pg, False))

        r, acc = lax.cond(jnp.max(r) > SB_SKIP, tail, lambda r, acc: (r, acc), r, acc)
    return acc


def _float_key(s):
    bits = lax.bitcast_convert_type(s, I32)
    return bits ^ (lax.shift_right_arithmetic(bits, 31) & 0x7FFFFFFF)


def _topk_threshold(keys_ref, n_chunks, topk, rows):
    width = keys_ref.shape[2]
    sub = min(rows, 16 * SUBLANES)

    starts = tuple(range(0, rows, sub))

    def bit_step(it, t_us):
        bit = 31 - it
        new = []
        for r0, t_u in zip(starts, t_us):
            cand = t_u | lax.shift_left(jnp.int32(1), bit)
            cand_s = jnp.broadcast_to(cand ^ INT_MIN, (sub, LANES))

            def count(c, acc, r0=r0, cand_s=cand_s):
                for t in range(width // LANES):
                    ge = keys_ref[c, r0:r0 + sub, t * LANES:(t + 1) * LANES] >= cand_s
                    acc = acc + jnp.where(ge, 1.0, 0.0)
                return acc

            acc = lax.fori_loop(0, n_chunks, count, jnp.zeros((sub, LANES), F32))
            cnt = jnp.sum(acc, axis=1, keepdims=True)
            new.append(jnp.where(cnt >= topk, cand, t_u))
        return tuple(new)

    t_us = lax.fori_loop(0, 32, bit_step, tuple(jnp.zeros((sub, 1), I32) for _ in starts))
    return jnp.concatenate(t_us, axis=0) ^ INT_MIN


def _softmax_update(lg, v_ext, m_ref, acc_ref, idx, n_sub, sub):
    ck = v_ext.shape[0]
    ps, alphas = [], []
    for i in range(n_sub):
        m_old = m_ref[idx(i)]
        m_new = jnp.maximum(m_old, jnp.max(lg[i], axis=1, keepdims=True))
        alphas.append(jnp.exp(m_old - m_new))
        ps.append(jnp.exp(lg[i] - jnp.concatenate([m_new] * (ck // LANES), axis=1)).astype(BF16))
        m_ref[idx(i)] = m_new
    pv = _dot(jnp.concatenate(ps, axis=0), v_ext)
    for i in range(n_sub):
        a2 = jnp.concatenate([alphas[i], alphas[i]], axis=1)
        acc_ref[idx(i)] = acc_ref[idx(i)] * a2 + pv[i * sub:(i + 1) * sub]


def _dsa_prompt_kernel(b31_ref, qi_ref, sm_ref, kk_ref, qa_ref, ka_ref, va_ref, t0_ref, t1_ref,
                       o_ref, keys_ref, qis_ref, qas_ref, m_ref, acc_ref, *, tq, topk):
    ck = tq
    rep = H_A // KV_A
    i = pl.program_id(1)
    w = sm_ref[:, :H_I] * IDX_SCALE
    row = lax.broadcasted_iota(I32, (tq, ck), 0)
    col = lax.broadcasted_iota(I32, (tq, ck), 1)
    causal = col <= row

    for p in range(H_I // 2):
        qis_ref[p * tq:(p + 1) * tq, :] = qi_ref[:, p * LANES:(p + 1) * LANES]
    for h in range(H_A):
        qas_ref[h * tq:(h + 1) * tq, :] = qa_ref[:, h * HEAD_DIM:(h + 1) * HEAD_DIM]

    def scores(c0):
        kd = kk_ref[pl.ds(c0, ck), :]
        kk = jnp.concatenate([kd[:, :LANES], kd[:, LANES:]], axis=0)
        s2 = _dot_nt(qis_ref[...], kk)
        s = jnp.zeros((tq, ck), F32)
        for p in range(H_I // 2):
            blk = s2[p * tq:(p + 1) * tq]
            s = s + jnp.maximum(blk[:, :ck], 0.0) * w[:, 2 * p:2 * p + 1]
            s = s + jnp.maximum(blk[:, ck:], 0.0) * w[:, 2 * p + 1:2 * p + 2]
        return s

    def score_chunk(c, carry):
        keys_ref[c] = _float_key(scores(pl.multiple_of(c * ck, ck)))
        return carry

    lax.fori_loop(0, i, score_chunk, 0)
    i0 = pl.multiple_of(i * ck, ck)
    keys_ref[i] = _float_key(jnp.where(causal, scores(i0), -jnp.inf))

    thr = _topk_threshold(keys_ref, i + 1, topk, tq)

    m_ref[...] = jnp.full(m_ref.shape, M_INIT, F32)
    acc_ref[...] = jnp.zeros_like(acc_ref)
    ones = jnp.ones((ck, HEAD_DIM), BF16)

    def attend(c0, sel, bias_fn):
        kc = ka_ref[pl.ds(c0, ck), :]
        vc = va_ref[pl.ds(c0, ck), :]
        for g in range(KV_A):
            lg_all = _dot_nt(qas_ref[g * rep * tq:(g + 1) * rep * tq, :],
                             kc[:, g * HEAD_DIM:(g + 1) * HEAD_DIM]) * ATT_SCALE
            lg = [jnp.where(sel, lg_all[r * tq:(r + 1) * tq] + bias_fn(g * rep + r), MASKED)
                  for r in range(rep)]
            v_ext = jnp.concatenate([vc[:, g * HEAD_DIM:(g + 1) * HEAD_DIM], ones], axis=1)
            _softmax_update(lg, v_ext, m_ref, acc_ref, lambda r: g * rep + r, rep, tq)

    def far_chunk(c, carry):
        attend(pl.multiple_of(c * ck, ck), keys_ref[c] >= thr, lambda h: b31_ref[h])
        return carry

    lax.fori_loop(0, i - 1, far_chunk, 0)

    @pl.when(i >= 1)
    def _():
        attend(pl.multiple_of((i - 1) * ck, ck), keys_ref[i - 1] >= thr, lambda h: t1_ref[h])

    attend(i0, jnp.logical_and(keys_ref[i] >= thr, causal), lambda h: t0_ref[h])

    for h in range(H_A):
        a = acc_ref[h]
        o_ref[:, h * HEAD_DIM:(h + 1) * HEAD_DIM] = (a[:, :HEAD_DIM] / a[:, HEAD_DIM:]).astype(o_ref.dtype)


def _bucket_table(n):
    d = np.arange(n)
    max_exact = NUM_BUCKETS // 2
    nf = np.maximum(d, 1).astype(np.float32)
    large = max_exact + (np.log(nf / np.float32(max_exact)) / np.float32(math.log(MAX_DISTANCE / max_exact))
                         * np.float32(NUM_BUCKETS - max_exact)).astype(np.int32)
    large = np.minimum(large, NUM_BUCKETS - 1)
    return np.where(d < max_exact, d, large).astype(np.int32)


def _bias_by_distance(rel_bias, dist):
    buckets = jnp.asarray(_bucket_table(int(dist.max()) + 1)[dist])
    out = jnp.zeros((rel_bias.shape[1],) + dist.shape, F32)
    for b in range(NUM_BUCKETS):
        out = jnp.where(buckets[None] == b, rel_bias[b].reshape((-1,) + (1,) * dist.ndim), out)
    return out


def _bias_tiles(rel_bias, tq):
    r = np.arange(tq)[:, None]
    j = np.arange(tq)[None, :]
    t0 = _bias_by_distance(rel_bias, np.maximum(r - j, 0))
    t1 = _bias_by_distance(rel_bias, tq + r - j)
    return t0, t1, rel_bias[NUM_BUCKETS - 1]


def _dsa_prompt(zf, zb, rel_bias, n_batch, seq, off_ka, off_va, off_sm, tq=256):
    topk = min(TOPK_MAX, seq // 4)
    nq = seq // tq
    t0, t1, b31 = _bias_tiles(rel_bias, tq)
    assert _bucket_table(4 * tq)[tq + 1:].min() == NUM_BUCKETS - 1
    return pl.pallas_call(
        functools.partial(_dsa_prompt_kernel, tq=tq, topk=topk),
        grid=(n_batch, nq),
        in_specs=[
            pl.BlockSpec(memory_space=pltpu.SMEM),
            pl.BlockSpec((tq, _QI), lambda b, i: (b * nq + i, OFF_QI // _QI)),
            pl.BlockSpec((tq, LANES), lambda b, i: (b * nq + i, off_sm // LANES + 2)),
            pl.BlockSpec((seq, 2 * LANES), lambda b, i: (b, off_sm // (2 * LANES))),
            pl.BlockSpec((tq, _QA), lambda b, i: (b * nq + i, OFF_QA // _QA)),
            pl.BlockSpec((seq, _KVA), lambda b, i: (b, off_ka // _KVA)),
            pl.BlockSpec((seq, _KVA), lambda b, i: (b, off_va // _KVA)),
            pl.BlockSpec((H_A, tq, tq), lambda b, i: (0, 0, 0)),
            pl.BlockSpec((H_A, tq, tq), lambda b, i: (0, 0, 0)),
        ],
        out_specs=pl.BlockSpec((tq, _QA), lambda b, i: (b * nq + i, 0)),
        out_shape=jax.ShapeDtypeStruct((n_batch * seq, _QA), BF16),
        scratch_shapes=[pltpu.VMEM((nq, tq, tq), I32),
                        pltpu.VMEM((H_I // 2 * tq, LANES), BF16),
                        pltpu.VMEM((H_A * tq, HEAD_DIM), BF16),
                        pltpu.VMEM((H_A, tq, LANES), F32),
                        pltpu.VMEM((H_A, tq, 2 * HEAD_DIM), F32)],
        compiler_params=_cparams(("parallel", "arbitrary")),
        name="dsa_prompt",
    )(b31, zb, zf, zb, zb, zb, zb, t0, t1)


def _idx_sample_kernel(pt_ref, q_ref, w_ref, kin_ref, *refs, pg):
    pages = refs[:pg]
    s_ref, snew_ref = refs[pg:]
    dec = s_ref.shape[2]
    q = q_ref[0]
    w = w_ref[0] * IDX_SCALE

    def score(k):
        s2 = _dot_nt(q, k.astype(BF16))
        s = jnp.zeros((dec, s2.shape[1]), F32)
        for h in range(H_I):
            s = s + jnp.maximum(s2[h * dec:(h + 1) * dec], 0.0) * w[h * dec:(h + 1) * dec]
        return s

    for k in range(pg):
        s_ref[0, k] = score(pages[k][0, 0])

    @pl.when(pl.program_id(1) == 0)
    def _():
        row = lax.broadcasted_iota(I32, snew_ref.shape[1:], 0)
        col = lax.broadcasted_iota(I32, snew_ref.shape[1:], 1)
        snew_ref[0] = jnp.where(col <= row, score(kin_ref[0]), -jnp.inf)


def _idx_sample(l, page_table, q, w, kin, pool_ki, pg):
    n_seq, n_pages = page_table.shape
    dec = q.shape[1] // H_I
    page = pool_ki.shape[2]
    full = lambda shp: pl.BlockSpec((1,) + shp, lambda b, j, pt: (b, 0, 0))
    page_blk = lambda k: pl.BlockSpec((1, 1, page, D_I), lambda b, j, pt: (l, pt[b, j * pg + k], 0, 0))
    grid_spec = pltpu.PrefetchScalarGridSpec(
        num_scalar_prefetch=1,
        grid=(n_seq, n_pages // pg),
        in_specs=[full(q.shape[1:]), full(w.shape[1:]), full(kin.shape[1:])] + [page_blk(k) for k in range(pg)],
        out_specs=[pl.BlockSpec((1, pg, dec, page), lambda b, j, pt: (b, j, 0, 0)),
                   pl.BlockSpec((1, dec, page), lambda b, j, pt: (b, 0, 0))],
    )
    return pl.pallas_call(
        functools.partial(_idx_sample_kernel, pg=pg),
        grid_spec=grid_spec,
        out_shape=[jax.ShapeDtypeStruct((n_seq, n_pages, dec, page), F32),
                   jax.ShapeDtypeStruct((n_seq, dec, page), F32)],
        compiler_params=_cparams(("parallel", "arbitrary")),
        name="idx_sample",
    )(page_table, q, w, kin, *([pool_ki] * pg))


def _dsa_sample_kernel(pt_ref, s_ref, snew_ref, qa_ref, kn_ref, vn_ref, bcol_ref, bpage_ref, bnew_ref,
                       *refs, pg, topk):
    kp, vp = refs[:pg], refs[pg:2 * pg]
    o_ref, keys_ref, thr_ref, m_ref, acc_ref = refs[2 * pg:]
    j = pl.program_id(1)
    last = pl.num_programs(1) - 1
    n_pages = s_ref.shape[1]
    dec, page = s_ref.shape[2], s_ref.shape[3]
    ones = jnp.ones((page, HEAD_DIM), BF16)

    @pl.when(j == 0)
    def _():
        def to_keys(p, carry):
            keys_ref[p] = _float_key(s_ref[0, p])
            return carry

        lax.fori_loop(0, n_pages, to_keys, 0)
        keys_ref[n_pages] = _float_key(snew_ref[0])
        thr = _topk_threshold(keys_ref, n_pages + 1, topk, dec)
        thr_ref[...] = jnp.broadcast_to(thr, thr_ref.shape)
        m_ref[...] = jnp.full(m_ref.shape, M_INIT, F32)
        acc_ref[...] = jnp.zeros_like(acc_ref)

    def tile_rows_cols(sb):
        sb = jnp.concatenate([sb] * (H_A), axis=0)
        return jnp.concatenate([sb] * KV_A, axis=1)

    def fold(k_groups, v_groups, bias):
        k_cat = jnp.concatenate([t for pg_tiles in k_groups for t in pg_tiles], axis=0)
        v_cat = jnp.concatenate([jnp.concatenate([t, ones], axis=1) for pg_tiles in v_groups for t in pg_tiles],
                                axis=0)
        lg = _dot_nt(qa_ref[0], k_cat) * ATT_SCALE + bias
        _softmax_update([lg], v_cat, m_ref, acc_ref, lambda _: slice(None), 1, qa_ref.shape[1])

    biases = []
    for k in range(pg):
        sb = tile_rows_cols(jnp.where(keys_ref[j * pg + k] >= thr_ref[...], 0.0, MASKED))
        if k == pg - 1:
            bias = jnp.where(j == last, bpage_ref[...], bcol_ref[...])
        else:
            bias = bcol_ref[...]
        biases.append(bias + sb)
    group = lambda ref, g: ref[0, 0, :, g * HEAD_DIM:(g + 1) * HEAD_DIM].astype(BF16)
    fold([[group(kp[k], g) for g in range(KV_A)] for k in range(pg)],
         [[group(vp[k], g) for g in range(KV_A)] for k in range(pg)],
         jnp.concatenate(biases, axis=1))

    @pl.when(j == last)
    def _():
        row = lax.broadcasted_iota(I32, (dec, page), 0)
        col = lax.broadcasted_iota(I32, (dec, page), 1)
        ok = jnp.logical_and(keys_ref[n_pages] >= thr_ref[...], col <= row)
        sb = tile_rows_cols(jnp.where(ok, 0.0, MASKED))
        kn = kn_ref[0].astype(BF16)
        vn = vn_ref[0].astype(BF16)
        fold([[kn[:, g * HEAD_DIM:(g + 1) * HEAD_DIM] for g in range(KV_A)]],
             [[vn[:, g * HEAD_DIM:(g + 1) * HEAD_DIM] for g in range(KV_A)]], bnew_ref[...] + sb)
        a = acc_ref[...]
        o_ref[0] = a[:, :HEAD_DIM] / a[:, HEAD_DIM:]


def _dsa_sample(l, page_table, s, snew, qa, kn, vn, biases, pool_k, pool_v, pg):
    n_seq, n_pages = page_table.shape
    dec, page = s.shape[2], s.shape[3]
    rows = qa.shape[1]
    topk = min(TOPK_MAX, (n_pages * page + dec) // 4)
    full2 = lambda a: pl.BlockSpec(a.shape, lambda b, j, pt: (0, 0))
    page_blk = lambda k: pl.BlockSpec((1, 1, page, _KVA), lambda b, j, pt: (l, pt[b, j * pg + k], 0, 0))
    grid_spec = pltpu.PrefetchScalarGridSpec(
        num_scalar_prefetch=1,
        grid=(n_seq, n_pages // pg),
        in_specs=[pl.BlockSpec((1, n_pages, dec, page), lambda b, j, pt: (b, 0, 0, 0)),
                  pl.BlockSpec((1, dec, page), lambda b, j, pt: (b, 0, 0)),
                  pl.BlockSpec((1, rows, HEAD_DIM), lambda b, j, pt: (b, 0, 0)),
                  pl.BlockSpec((1, page, _KVA), lambda b, j, pt: (b, 0, 0)),
                  pl.BlockSpec((1, page, _KVA), lambda b, j, pt: (b, 0, 0))]
                 + [full2(a) for a in biases] + [page_blk(k) for k in range(pg)] * 2,
        out_specs=pl.BlockSpec((1, rows, HEAD_DIM), lambda b, j, pt: (b, 0, 0)),
        scratch_shapes=[pltpu.VMEM((n_pages + 1, dec, page), I32),
                        pltpu.VMEM((dec, page), I32),
                        pltpu.VMEM((rows, LANES), F32),
                        pltpu.VMEM((rows, 2 * HEAD_DIM), F32)],
    )
    return pl.pallas_call(
        functools.partial(_dsa_sample_kernel, pg=pg, topk=topk),
        grid_spec=grid_spec,
        out_shape=jax.ShapeDtypeStruct((n_seq, rows, HEAD_DIM), F32),
        compiler_params=_cparams(("parallel", "arbitrary")),
        name="dsa_sample",
    )(page_table, s, snew, qa, kn, vn, *biases, *([pool_k] * pg), *([pool_v] * pg))


def _sample_bias(rel_bias, dec, page):
    assert _bucket_table(4 * page)[page + 1:].min() == NUM_BUCKETS - 1
    t = np.arange(dec)[:, None]
    jc = np.arange(page)[None, :]
    rep = H_A // KV_A
    row_g = np.arange(H_A * dec)[:, None] // (rep * dec)
    col_g = np.arange(KV_A * page)[None, :] // page
    other = jnp.asarray(np.where(row_g == col_g, 0.0, MASKED), F32)

    def expand(b):
        b = b.reshape(H_A * dec, page)
        return jnp.concatenate([b] * KV_A, axis=1) + other

    bpage = expand(_bias_by_distance(rel_bias, page + t - jc))
    bnew = expand(_bias_by_distance(rel_bias, np.maximum(t - jc, 0)))
    bcol = expand(jnp.broadcast_to(rel_bias[NUM_BUCKETS - 1][:, None, None], (H_A, dec, page)))
    return bcol, bpage, bnew


def _merge_kernel(oa_ref, ob_ref, wa_ref, wb_ref, ga_ref, gb_ref, m_ref):
    pa = _dot(oa_ref[...], wa_ref[...].astype(BF16))
    pb = _dot(ob_ref[...], wb_ref[...].astype(BF16))
    m_ref[...] = (jax.nn.sigmoid(ga_ref[...]) * pa + jax.nn.sigmoid(gb_ref[...]) * pb).astype(m_ref.dtype)


def _merge(oa, ob, w_pa, w_pb, zf, off_ga, off_gb, tm, tn=512):
    m, ka = oa.shape
    kb = ob.shape[1]
    n = w_pa.shape[1]
    return pl.pallas_call(
        _merge_kernel,
        grid=(m // tm, n // tn),
        in_specs=[pl.BlockSpec((tm, ka), lambda i, j: (i, 0)),
                  pl.BlockSpec((tm, kb), lambda i, j: (i, 0)),
                  pl.BlockSpec((ka, tn), lambda i, j: (0, j)),
                  pl.BlockSpec((kb, tn), lambda i, j: (0, j)),
                  pl.BlockSpec((tm, tn), lambda i, j: (i, off_ga // tn + j)),
                  pl.BlockSpec((tm, tn), lambda i, j: (i, off_gb // tn + j))],
        out_specs=pl.BlockSpec((tm, tn), lambda i, j: (i, j)),
        out_shape=jax.ShapeDtypeStruct((m, n), BF16),
        compiler_params=_cparams(("parallel", "arbitrary")),
        name="merge",
    )(oa, ob, w_pa, w_pb, zf, zf)


def _mm_ln_kernel(x_ref, a_ref, w_ref, g_ref, b_ref, yf_ref, yb_ref, acc_ref, *, alpha):
    k = pl.program_id(1)

    @pl.when(k == 0)
    def _():
        acc_ref[...] = jnp.zeros_like(acc_ref)

    acc_ref[...] += _dot(a_ref[...].astype(BF16), w_ref[...])

    @pl.when(k == pl.num_programs(1) - 1)
    def _():
        y = alpha * x_ref[...] + acc_ref[...]
        mu = jnp.mean(y, axis=-1, keepdims=True)
        yc = y - mu
        var = jnp.mean(yc * yc, axis=-1, keepdims=True)
        out = yc * lax.rsqrt(var + LN_EPS) * g_ref[...] + b_ref[...]
        yf_ref[...] = out
        yb_ref[...] = out.astype(BF16)


def _mm_ln(x, a, w, g, b, alpha, tm):
    m, n = x.shape
    kdim = a.shape[1]
    tk = _largest_tile(kdim, 1408)
    return pl.pallas_call(
        functools.partial(_mm_ln_kernel, alpha=alpha),
        grid=(m // tm, kdim // tk),
        in_specs=[pl.BlockSpec((tm, n), lambda i, k: (i, 0)),
                  pl.BlockSpec((tm, tk), lambda i, k: (i, k)),
                  pl.BlockSpec((tk, n), lambda i, k: (k, 0)),
                  pl.BlockSpec((1, n), lambda i, k: (0, 0)),
                  pl.BlockSpec((1, n), lambda i, k: (0, 0))],
        out_specs=[pl.BlockSpec((tm, n), lambda i, k: (i, 0)),
                   pl.BlockSpec((tm, n), lambda i, k: (i, 0))],
        out_shape=[jax.ShapeDtypeStruct((m, n), F32), jax.ShapeDtypeStruct((m, n), BF16)],
        scratch_shapes=[pltpu.VMEM((tm, n), F32)],
        compiler_params=_cparams(("parallel", "arbitrary")),
        name="mm_ln",
    )(x, a, w, g.reshape(1, n), b.reshape(1, n))


def _gelu_tanh(x):
    return 0.5 * x * (1.0 + jnp.tanh(math.sqrt(2.0 / math.pi) * (x + 0.044715 * (x * x * x))))


def _ffn_up_kernel(x_ref, wa_ref, wv_ref, cw_ref, cb_ref, prev_ref, h_ref, tail_ref, ext_ref,
                   *, tm, tiles_per_seq):
    i = pl.program_id(1)
    wa = wa_ref[...].astype(BF16)
    wv = wv_ref[...].astype(BF16)
    cw = cw_ref[...]
    cb = cb_ref[...]

    @pl.when(i % tiles_per_seq == 0)
    def _():
        ext_ref[0:SUBLANES, :] = prev_ref[0]

    rc = min(tm, 256)
    for r0 in range(0, tm, rc):
        x = x_ref[r0:r0 + rc, :].astype(BF16)
        a = _dot(x, wa)
        val = _dot(x, wv)
        e0 = SUBLANES + r0
        ext_ref[e0:e0 + rc, :] = a
        c = (cb
             + cw[0:1, :] * ext_ref[e0 - 2:e0 - 2 + rc, :]
             + cw[1:2, :] * ext_ref[e0 - 1:e0 - 1 + rc, :]
             + cw[2:3, :] * a)
        h_ref[r0:r0 + rc, :] = (_gelu_tanh(c) * val).astype(h_ref.dtype)
    last = ext_ref[tm:tm + SUBLANES, :]
    tail_ref[0] = last
    ext_ref[0:SUBLANES, :] = last


def _ffn_up(x, w_up, conv_w, conv_b, prev8, tm, rows_per_seq, h_dtype, tf=512):
    m, d = x.shape
    f = conv_w.shape[1]
    nf = f // tf
    tiles_per_seq = rows_per_seq // tm
    return pl.pallas_call(
        functools.partial(_ffn_up_kernel, tm=tm, tiles_per_seq=tiles_per_seq),
        grid=(nf, m // tm),
        in_specs=[pl.BlockSpec((tm, d), lambda j, i: (i, 0)),
                  pl.BlockSpec((d, tf), lambda j, i: (0, j)),
                  pl.BlockSpec((d, tf), lambda j, i: (0, nf + j)),
                  pl.BlockSpec((CONV_W, tf), lambda j, i: (0, j)),
                  pl.BlockSpec((1, tf), lambda j, i: (0, j)),
                  pl.BlockSpec((1, SUBLANES, tf), lambda j, i: (i // tiles_per_seq, 0, j))],
        out_specs=[pl.BlockSpec((tm, tf), lambda j, i: (i, j)),
                   pl.BlockSpec((1, SUBLANES, tf), lambda j, i: (i // tiles_per_seq, 0, j))],
        out_shape=[jax.ShapeDtypeStruct((m, f), h_dtype),
                   jax.ShapeDtypeStruct((m // rows_per_seq, SUBLANES, f), F32)],
        scratch_shapes=[pltpu.VMEM((tm + SUBLANES, tf), F32)],
        compiler_params=_cparams(("parallel", "arbitrary")),
        name="ffn_up",
    )(x, w_up, w_up, conv_w, conv_b.reshape(1, f), prev8)


def _split_heads_kernel(z_ref, o_ref):
    for h in range(o_ref.shape[1]):
        o_ref[:, h, :] = z_ref[:, h * HEAD_DIM:(h + 1) * HEAD_DIM]


def _split_heads(zf, off, n_heads, tm=512):
    m = zf.shape[0]
    width = n_heads * HEAD_DIM
    return pl.pallas_call(
        _split_heads_kernel,
        grid=(m // tm,),
        in_specs=[pl.BlockSpec((tm, width), lambda i: (i, off // width))],
        out_specs=pl.BlockSpec((tm, n_heads, HEAD_DIM), lambda i: (i, 0, 0)),
        out_shape=jax.ShapeDtypeStruct((m, n_heads, HEAD_DIM), F32),
        compiler_params=_cparams(("parallel",)),
        name="split_heads",
    )(zf)


def _pad_w_in(w_in):
    sizes = (_QA, _KVA, _KVA, _QI, D_I, H_I, _QB, _QB, _QB)
    offs = np.cumsum((0,) + sizes)
    qa, ka, va, qi, ki, wi, qb, kb, vb = [w_in[..., offs[t]:offs[t + 1]] for t in range(9)]
    g = w_in[..., offs[9]:]
    zk = jnp.zeros_like(ki)
    pad = jnp.zeros(w_in.shape[:-1] + (SMALL_W - 4 * D_I - H_I,), w_in.dtype)
    return jnp.concatenate([qa, qi, qb, kb, vb, g, ka, va, ki, zk, zk, ki, wi, pad], axis=-1).astype(BF16)


def _pad_rows(a, rows):
    return jnp.pad(a, ((0, 0), (0, rows - a.shape[1]), (0, 0)))


def kernel(x_prompt, x_sample, cache_k_a, cache_v_a, cache_k_idx, cache_k_b, cache_v_b, state_conv,
           page_table, rel_bias, w_in, w_proj_a, w_proj_b, w_out, ln1_g, ln1_b, w_up, conv_w, conv_b,
           w_down, ln2_g, ln2_b):
    depth = w_in.shape[0]
    n_p, seq, d_model = x_prompt.shape
    n_s, dec = x_sample.shape[:2]
    d_ff = conv_w.shape[-1]
    n_pool, page = cache_k_a.shape[1:3]
    n_pages = page_table.shape[1]
    rep = H_A // KV_A
    alpha = (2 * depth) ** 0.25
    assert dec == SUBLANES and page == LANES

    off_ga = OFF_VB + _QB
    off_gb = off_ga + d_model
    off_ka = off_gb + d_model
    off_va = off_ka + _KVA
    off_sm = off_va + _KVA
    off_wi = off_sm + 2 * LANES
    w_in_p = _pad_w_in(w_in)
    p_pad = w_in_p.shape[-1]
    tn_in = _largest_tile(p_pad, 1024)
    w_out_b = w_out.astype(BF16)
    w_down_b = w_down.astype(BF16)

    pool_ka = cache_k_a.reshape(depth, n_pool, page, _KVA)
    pool_va = cache_v_a.reshape(depth, n_pool, page, _KVA)
    pool_kb, pool_vb = cache_k_b, cache_v_b
    pg_a = max(t for t in (16, 8, 4, 2, 1) if n_pages % t == 0)
    sample_bias = _sample_bias(rel_bias, dec, page)

    mp = n_p * seq
    ms = n_s * dec
    xp_f = x_prompt.reshape(mp, d_model)
    xp_b = xp_f.astype(BF16)
    xs_f = x_sample.reshape(ms, d_model)
    tm_p = min(1024, seq)
    tm_ln = min(512, seq)
    zero_prev = jnp.zeros((n_p, SUBLANES, d_ff), F32)

    outs = [[] for _ in range(12)]
    for l in range(depth):
        zf, zb = _inproj(xp_b, w_in_p[l], tm_p, tn_in)
        oa = _dsa_prompt(zf, zb, rel_bias, n_p, seq, off_ka, off_va, off_sm)
        ob = _sb_prompt(zb, n_p, seq)
        mg = _merge(oa, ob, w_proj_a[l], w_proj_b[l], zf, off_ga, off_gb, tm_p)
        xp_f, xp_b = _mm_ln(xp_f, mg, w_out_b[l], ln1_g[l], ln1_b[l], alpha, tm_ln)
        h, tail = _ffn_up(xp_b, w_up[l], conv_w[l], conv_b[l], zero_prev, tm_p, seq, BF16)
        xp_f, xp_b = _mm_ln(xp_f, h, w_down_b[l], ln2_g[l], ln2_b[l], alpha, tm_ln)
        outs[0].append(_split_heads(zf, off_ka, KV_A, tm_ln).reshape(n_p, seq, KV_A, HEAD_DIM))
        outs[1].append(_split_heads(zf, off_va, KV_A, tm_ln).reshape(n_p, seq, KV_A, HEAD_DIM))
        outs[2].append(zf[:, off_sm:off_sm + D_I].reshape(n_p, seq, D_I))
        outs[3].append(_split_heads(zf, OFF_KB, H_B, tm_ln).reshape(n_p, seq, H_B, HEAD_DIM))
        outs[4].append(_split_heads(zf, OFF_VB, H_B, tm_ln).reshape(n_p, seq, H_B, HEAD_DIM))
        outs[5].append(tail[:, SUBLANES - (CONV_W - 1):, :])

        zs, _ = _inproj(xs_f, w_in_p[l], ms, tn_in)
        z3 = zs.reshape(n_s, dec, p_pad)
        ka = z3[..., off_ka:off_ka + _KVA]
        va = z3[..., off_va:off_va + _KVA]
        ki = z3[..., off_sm:off_sm + D_I]
        kb = z3[..., OFF_KB:OFF_KB + _QB]
        vb = z3[..., OFF_VB:OFF_VB + _QB]
        qi_s = (z3[..., OFF_QI:OFF_QI + _QI].reshape(n_s, dec, H_I, D_I).transpose(0, 2, 1, 3)
                .reshape(n_s, H_I * dec, D_I).astype(BF16))
        wi_s = z3[..., off_wi:off_wi + H_I].transpose(0, 2, 1).reshape(n_s, H_I * dec, 1)
        qa_s = (z3[..., OFF_QA:OFF_QA + _QA].reshape(n_s, dec, KV_A, rep, HEAD_DIM).transpose(0, 2, 3, 1, 4)
                .reshape(n_s, H_A * dec, HEAD_DIM).astype(BF16))
        s_past, s_new = _idx_sample(l, page_table, qi_s, wi_s, _pad_rows(ki, page), cache_k_idx, pg_a)
        oa_s = _dsa_sample(l, page_table, s_past, s_new, qa_s, _pad_rows(ka, page), _pad_rows(va, page),
                           sample_bias, pool_ka, pool_va, pg_a)
        oa_s = (oa_s.reshape(n_s, KV_A, rep, dec, HEAD_DIM).transpose(0, 3, 1, 2, 4)
                .reshape(ms, _QA).astype(BF16))
        ob_s = _sb_sample(l, page_table, z3[..., OFF_QB:OFF_QB + _QB].reshape(n_s, dec * H_B, HEAD_DIM),
                          kb.reshape(n_s, dec * H_B, HEAD_DIM), vb.reshape(n_s, dec * H_B, HEAD_DIM),
                          pool_kb, pool_vb)
        ob_s = ob_s.reshape(ms, _QB).astype(BF16)
        mg_s = _merge(oa_s, ob_s, w_proj_a[l], w_proj_b[l], zs, off_ga, off_gb, ms)
        xs_f, _ = _mm_ln(xs_f, mg_s, w_out_b[l], ln1_g[l], ln1_b[l], alpha, ms)
        prev8 = jnp.concatenate([jnp.zeros((n_s, SUBLANES - (CONV_W - 1), d_ff), F32), state_conv[l]], axis=1)
        h_s, tail_s = _ffn_up(xs_f, w_up[l], conv_w[l], conv_b[l], prev8, dec, dec, F32)
        xs_f, _ = _mm_ln(xs_f, h_s, w_down_b[l], ln2_g[l], ln2_b[l], alpha, ms)
        outs[6].append(ka.reshape(n_s, dec, KV_A, HEAD_DIM))
        outs[7].append(va.reshape(n_s, dec, KV_A, HEAD_DIM))
        outs[8].append(ki)
        outs[9].append(kb.reshape(n_s, dec, H_B, HEAD_DIM))
        outs[10].append(vb.reshape(n_s, dec, H_B, HEAD_DIM))
        outs[11].append(tail_s[:, SUBLANES - (CONV_W - 1):, :])

    return (xp_f.reshape(n_p, seq, d_model), xs_f.reshape(n_s, dec, d_model)) + tuple(jnp.stack(o) for o in outs)
```
